```python
import math
import jax
import jax.numpy as jnp
from jax import lax
import numpy as np

D_MODEL = 1024
BATCH = 2
SEQ = 8192
DEPTH = 1
DEC_BATCH = 128
DEC_SEQ = 4
PAST_LEN = 8192
PAGE_SIZE = 128

HEAD_DIM = 64
FOX_HEADS = 8
DIFF_HEADS = 4
MEM_HEADS = 4
MEM_HEAD_DIM = 128
N_MEM = 256
N_BRANCH = 3
FOX_WIDTH = FOX_HEADS * HEAD_DIM
DIFF_QK_WIDTH = 2 * DIFF_HEADS * HEAD_DIM
DIFF_V_WIDTH = DIFF_HEADS * 2 * HEAD_DIM
MEM_WIDTH = MEM_HEADS * MEM_HEAD_DIM
SPLIT_SIZES = (FOX_WIDTH, FOX_WIDTH, FOX_WIDTH, FOX_HEADS, DIFF_QK_WIDTH, DIFF_QK_WIDTH, DIFF_V_WIDTH, MEM_WIDTH, N_BRANCH * D_MODEL)
D_IN = 3 * FOX_WIDTH + FOX_HEADS + 2 * DIFF_QK_WIDTH + DIFF_V_WIDTH + MEM_WIDTH + N_BRANCH * D_MODEL
Q_BLOCK = 128
ATTN_SCALE = HEAD_DIM ** -0.5
MEM_SCALE = MEM_HEAD_DIM ** -0.5
PEER_HEADS = 8
PEER_TOPK = 16
N_KEYS = 128
N_EXPERTS = N_KEYS * N_KEYS
PEER_DK = 256
PEER_BLOCK = 128
RMS_EPS = 1e-6

kernel_name = "hybrid_fox_diff_mem_peer_step"


def rms_norm(x, g):
    xf = x.astype(jnp.float32)
    y = xf * lax.rsqrt(jnp.mean(xf * xf, axis=-1, keepdims=True) + RMS_EPS)
    return (y * g.astype(jnp.float32)).astype(x.dtype)


def lambda_init(layer):
    return 0.8 - 0.6 * math.exp(-0.3 * layer)


def diff_lambda_value(dl, lam_init):
    dl = dl.astype(jnp.float32)
    return jnp.exp(jnp.sum(dl[0] * dl[1])) - jnp.exp(jnp.sum(dl[2] * dl[3])) + lam_init


def alibi_slopes():
    s = jnp.exp2(-8.0 * jnp.arange(1, DIFF_HEADS + 1, dtype=jnp.float32) / DIFF_HEADS)
    return jnp.tile(s, 2)


def split_projection(xn, w_in, b_f, fox_q_norm, fox_k_norm, diff_q_norm, diff_k_norm, mem_q_norm):
    B, T, _ = xn.shape
    z = jnp.einsum('btd,de->bte', xn, w_in)
    cuts = [int(c) for c in np.cumsum(SPLIT_SIZES)[:-1]]
    fq, fk, fv, ff, dq, dk, dv, mq, gz = jnp.split(z, cuts, axis=-1)
    fq = rms_norm(fq.reshape(B, T, FOX_HEADS, HEAD_DIM), fox_q_norm)
    fk = rms_norm(fk.reshape(B, T, FOX_HEADS, HEAD_DIM), fox_k_norm)
    fv = fv.reshape(B, T, FOX_HEADS, HEAD_DIM)
    logf = jax.nn.log_sigmoid(ff.astype(jnp.float32) + b_f.astype(jnp.float32))
    dq = rms_norm(dq.reshape(B, T, 2 * DIFF_HEADS, HEAD_DIM), diff_q_norm)
    dk = rms_norm(dk.reshape(B, T, 2 * DIFF_HEADS, HEAD_DIM), diff_k_norm)
    dv = dv.reshape(B, T, DIFF_HEADS, 2 * HEAD_DIM)
    mq = rms_norm(mq.reshape(B, T, MEM_HEADS, MEM_HEAD_DIM), mem_q_norm)
    gates = jax.nn.sigmoid(gz).reshape(B, T, N_BRANCH, D_MODEL)
    return fq, fk, fv, logf, dq, dk, dv, mq, gates


def memory_kv(mem, norm_mem, w_mem_kv, mem_k_norm):
    B, M, _ = mem.shape
    mn = rms_norm(mem, norm_mem)
    mk, mv = jnp.split(jnp.einsum('bmd,de->bme', mn, w_mem_kv), 2, axis=-1)
    mk = rms_norm(mk.reshape(B, M, MEM_HEADS, MEM_HEAD_DIM), mem_k_norm)
    mv = mv.reshape(B, M, MEM_HEADS, MEM_HEAD_DIM)
    return mk, mv


def memory_attention(mq, mk, mv):
    s = jnp.einsum('bthd,bmhd->bhtm', mq, mk).astype(jnp.float32) * MEM_SCALE
    p = jax.nn.softmax(s, axis=-1)
    return jnp.einsum('bhtm,bmhd->bthd', p.astype(mv.dtype), mv)


def fox_prompt(q, k, v, logf):
    B, S, H, d = q.shape
    Ft = jnp.moveaxis(jnp.cumsum(logf, axis=1), 2, 1)
    kpos = jnp.arange(S)

    def block(i):
        start = i * Q_BLOCK
        qb = lax.dynamic_slice_in_dim(q, start, Q_BLOCK, 1)
        Fq = lax.dynamic_slice_in_dim(Ft, start, Q_BLOCK, 2)
        qpos = start + jnp.arange(Q_BLOCK)
        s = jnp.einsum('bqhd,bkhd->bhqk', qb, k).astype(jnp.float32) * ATTN_SCALE + Fq[..., :, None] - Ft[..., None, :]
        s = jnp.where(kpos[None, :] <= qpos[:, None], s, -jnp.inf)
        p = jax.nn.softmax(s, axis=-1)
        return jnp.einsum('bhqk,bkhd->bqhd', p.astype(v.dtype), v)

    o = lax.map(block, jnp.arange(S // Q_BLOCK))
    return jnp.moveaxis(o, 0, 1).reshape(B, S, H, d)


def diff_prompt(q, k, v, lam, slopes):
    B, S, _, _ = q.shape
    kpos = jnp.arange(S)

    def block(i):
        start = i * Q_BLOCK
        qb = lax.dynamic_slice_in_dim(q, start, Q_BLOCK, 1)
        dist = (start + jnp.arange(Q_BLOCK))[:, None] - kpos[None, :]
        s = jnp.einsum('bqhd,bkhd->bhqk', qb, k).astype(jnp.float32) * ATTN_SCALE
        s = s - slopes[:, None, None] * dist.astype(jnp.float32)
        s = jnp.where(dist >= 0, s, -jnp.inf)
        p = jax.nn.softmax(s, axis=-1).reshape(B, 2, DIFF_HEADS, Q_BLOCK, S)
        w = p[:, 0] - lam * p[:, 1]
        return jnp.einsum('bhqk,bkhd->bqhd', w.astype(v.dtype), v)

    o = lax.map(block, jnp.arange(S // Q_BLOCK))
    return jnp.moveaxis(o, 0, 1).reshape(B, S, DIFF_HEADS, 2 * HEAD_DIM)


def paged_scores(q, k_pool, layer, page_table):
    B, T, H, _ = q.shape

    def one_page(idx):
        return jnp.einsum('bthd,bphd->bhtp', q, k_pool[layer, idx]).astype(jnp.float32)

    s = lax.map(one_page, page_table.T)
    return jnp.moveaxis(s, 0, 3).reshape(B, H, T, -1)


def paged_values(w, v_pool, layer, page_table):
    B, H, T, _ = w.shape
    n_pages = page_table.shape[1]
    wp = jnp.moveaxis(w.reshape(B, H, T, n_pages, PAGE_SIZE), 3, 0)
    dv = v_pool.shape[-1]

    def step(acc, xs):
        idx, wb = xs
        return acc + jnp.einsum('bhtp,bphd->bthd', wb, v_pool[layer, idx].astype(jnp.float32)), None

    acc, _ = lax.scan(step, jnp.zeros((B, T, H, dv), jnp.float32), (page_table.T, wp))
    return acc


def fox_sample(q, k_new, v_new, logf_new, k_pool, v_pool, logf_pool, layer, page_table):
    B, T, H, _ = q.shape
    logf_past = logf_pool[layer, page_table].reshape(B, -1, H).astype(jnp.float32)
    P = logf_past.shape[1]
    Ft = jnp.moveaxis(jnp.cumsum(jnp.concatenate([logf_past, logf_new], axis=1), axis=1), 2, 1)
    Fq = Ft[:, :, P:]
    s_past = paged_scores(q, k_pool, layer, page_table) * ATTN_SCALE + Fq[..., :, None] - Ft[:, :, None, :P]
    s_new = jnp.einsum('bthd,bshd->bhts', q, k_new).astype(jnp.float32) * ATTN_SCALE + Fq[..., :, None] - Fq[..., None, :]
    causal = jnp.arange(T)[:, None] >= jnp.arange(T)[None, :]
    s_new = jnp.where(causal, s_new, -jnp.inf)
    p = jax.nn.softmax(jnp.concatenate([s_past, s_new], axis=-1), axis=-1)
    o = paged_values(p[..., :P], v_pool, layer, page_table) + jnp.einsum('bhts,bshd->bthd', p[..., P:], v_new.astype(jnp.float32))
    return o.astype(q.dtype)


def diff_sample(q, k_new, v_new, k_pool, v_pool, layer, page_table, lam, slopes):
    B, T, _, _ = q.shape
    P = page_table.shape[1] * PAGE_SIZE
    dist_past = ((P + jnp.arange(T))[:, None] - jnp.arange(P)[None, :]).astype(jnp.float32)
    s_past = paged_scores(q, k_pool, layer, page_table) * ATTN_SCALE - slopes[:, None, None] * dist_past
    dist_new = jnp.arange(T)[:, None] - jnp.arange(T)[None, :]
    s_new = jnp.einsum('bthd,bshd->bhts', q, k_new).astype(jnp.float32) * ATTN_SCALE - slopes[:, None, None] * dist_new.astype(jnp.float32)
    s_new = jnp.where(dist_new >= 0, s_new, -jnp.inf)
    p = jax.nn.softmax(jnp.concatenate([s_past, s_new], axis=-1), axis=-1).reshape(B, 2, DIFF_HEADS, T, P + T)
    w = p[:, 0] - lam * p[:, 1]
    o = paged_values(w[..., :P], v_pool, layer, page_table) + jnp.einsum('bhts,bshd->bthd', w[..., P:], v_new.astype(jnp.float32))
    return o.astype(q.dtype)


def diff_post(o, g, lam_init):
    return rms_norm(o, g) * (1.0 - lam_init)


def merge_branches(gates, o_fox, o_diff, o_mem, w_proj_fox, w_proj_diff, w_proj_mem, w_out):
    B, T = o_fox.shape[:2]
    b_fox = jnp.einsum('bte,ed->btd', o_fox.reshape(B, T, -1), w_proj_fox)
    b_diff = jnp.einsum('bte,ed->btd', o_diff.reshape(B, T, -1), w_proj_diff)
    b_mem = jnp.einsum('bte,ed->btd', o_mem.reshape(B, T, -1), w_proj_mem)
    m = gates[:, :, 0] * b_fox + gates[:, :, 1] * b_diff + gates[:, :, 2] * b_mem
    return jnp.einsum('btd,de->bte', m, w_out)


def peer_ffn(xn, w_peer_q, peer_subkeys, peer_u, peer_v):
    B, T, D = xn.shape
    n_tok = B * T
    n_blk = -(-n_tok // PEER_BLOCK)
    xf = jnp.pad(xn.reshape(n_tok, D), ((0, n_blk * PEER_BLOCK - n_tok), (0, 0)))

    def one(xb):
        q = jnp.einsum('td,de->te', xb, w_peer_q).reshape(-1, PEER_HEADS, 2, PEER_DK // 2)
        s = jnp.einsum('thcd,ckd->thck', q, peer_subkeys).astype(jnp.float32)
        top_s, top_i = lax.top_k(s, PEER_TOPK)
        cand = top_s[:, :, 0, :, None] + top_s[:, :, 1, None, :]
        best, ci = lax.top_k(cand.reshape(cand.shape[0], PEER_HEADS, PEER_TOPK * PEER_TOPK), PEER_TOPK)
        i1 = jnp.take_along_axis(top_i[:, :, 0], ci // PEER_TOPK, axis=-1)
        i2 = jnp.take_along_axis(top_i[:, :, 1], ci % PEER_TOPK, axis=-1)
        eidx = i1 * N_KEYS + i2
        g = jax.nn.softmax(best, axis=-1)
        act = jax.nn.gelu(jnp.einsum('td,thkd->thk', xb, peer_u[eidx]).astype(jnp.float32), approximate=False)
        return jnp.einsum('thk,thkd->td', (g * act).astype(xb.dtype), peer_v[eidx])

    out = lax.map(one, xf.reshape(n_blk, PEER_BLOCK, D))
    return out.reshape(-1, D)[:n_tok].reshape(B, T, D)


def setup_inputs(seed: int = 0) -> dict:
    key = jax.random.key(seed)
    ks = iter(jax.random.split(key, 48))
    f32 = jnp.float32

    def nrm(shape, scale=1.0):
        return jax.random.normal(next(ks), shape, f32) * scale

    def gain(shape):
        return 1.0 + 0.02 * nrm(shape)

    n_pages = PAST_LEN // PAGE_SIZE
    n_used = DEC_BATCH * n_pages
    n_pool = n_used + (n_used + 3) // 4
    page_table = jax.random.permutation(next(ks), n_pool)[:n_used].reshape(DEC_BATCH, n_pages).astype(jnp.int32)
    return {
        'x_prompt': nrm((BATCH, SEQ, D_MODEL)),
        'x_sample': nrm((DEC_BATCH, DEC_SEQ, D_MODEL)),
        'mem_prompt': nrm((BATCH, N_MEM, D_MODEL)),
        'cache_fox_k': nrm((DEPTH, n_pool, PAGE_SIZE, FOX_HEADS, HEAD_DIM)),
        'cache_fox_v': nrm((DEPTH, n_pool, PAGE_SIZE, FOX_HEADS, HEAD_DIM)),
        'cache_fox_logf': jax.nn.log_sigmoid(3.0 + nrm((DEPTH, n_pool, PAGE_SIZE, FOX_HEADS))),
        'cache_diff_k': nrm((DEPTH, n_pool, PAGE_SIZE, 2 * DIFF_HEADS, HEAD_DIM)),
        'cache_diff_v': nrm((DEPTH, n_pool, PAGE_SIZE, DIFF_HEADS, 2 * HEAD_DIM)),
        'cache_mem_k': nrm((DEPTH, DEC_BATCH, N_MEM, MEM_HEADS, MEM_HEAD_DIM)),
        'cache_mem_v': nrm((DEPTH, DEC_BATCH, N_MEM, MEM_HEADS, MEM_HEAD_DIM)),
        'page_table': page_table,
        'norm_mix': gain((DEPTH, D_MODEL)),
        'norm_mem': gain((DEPTH, D_MODEL)),
        'w_in': nrm((DEPTH, D_MODEL, D_IN), D_MODEL ** -0.5),
        'b_f': 1.0 + 4.0 * jax.random.uniform(next(ks), (DEPTH, FOX_HEADS), f32),
        'fox_q_norm': gain((DEPTH, HEAD_DIM)),
        'fox_k_norm': gain((DEPTH, HEAD_DIM)),
        'diff_q_norm': gain((DEPTH, HEAD_DIM)),
        'diff_k_norm': gain((DEPTH, HEAD_DIM)),
        'diff_lambda': nrm((DEPTH, 4, HEAD_DIM), 0.1),
        'diff_out_norm': gain((DEPTH, 2 * HEAD_DIM)),
        'w_mem_kv': nrm((DEPTH, D_MODEL, 2 * MEM_WIDTH), D_MODEL ** -0.5),
        'mem_q_norm': gain((DEPTH, MEM_HEAD_DIM)),
        'mem_k_norm': gain((DEPTH, MEM_HEAD_DIM)),
        'w_proj_fox': nrm((DEPTH, FOX_WIDTH, D_MODEL), FOX_WIDTH ** -0.5),
        'w_proj_diff': nrm((DEPTH, DIFF_V_WIDTH, D_MODEL), DIFF_V_WIDTH ** -0.5),
        'w_proj_mem': nrm((DEPTH, MEM_WIDTH, D_MODEL), MEM_WIDTH ** -0.5),
        'w_out': nrm((DEPTH, D_MODEL, D_MODEL), D_MODEL ** -0.5),
        'norm_ffn': gain((DEPTH, D_MODEL)),
        'w_peer_q': nrm((DEPTH, D_MODEL, PEER_HEADS * PEER_DK), D_MODEL ** -0.5),
        'peer_subkeys': nrm((DEPTH, 2, N_KEYS, PEER_DK // 2), (PEER_DK // 2) ** -0.5),
        'peer_u': nrm((DEPTH, N_EXPERTS, D_MODEL), D_MODEL ** -0.5),
        'peer_v': nrm((DEPTH, N_EXPERTS, D_MODEL), PEER_HEADS ** -0.5),
    }


def reference(x_prompt, x_sample, mem_prompt, cache_fox_k, cache_fox_v, cache_fox_logf, cache_diff_k, cache_diff_v,
              cache_mem_k, cache_mem_v, page_table, norm_mix, norm_mem, w_in, b_f, fox_q_norm, fox_k_norm,
              diff_q_norm, diff_k_norm, diff_lambda, diff_out_norm, w_mem_kv, mem_q_norm, mem_k_norm,
              w_proj_fox, w_proj_diff, w_proj_mem, w_out, norm_ffn, w_peer_q, peer_subkeys, peer_u, peer_v):
    slopes = alibi_slopes()
    xp, xs = x_prompt, x_sample
    p_fk, p_fv, p_fl, p_dk, p_dv, p_mk, p_mv = [], [], [], [], [], [], []
    s_fk, s_fv, s_fl, s_dk, s_dv = [], [], [], [], []
    for l in range(DEPTH):
        lam_init = lambda_init(l)
        lam = diff_lambda_value(diff_lambda[l], lam_init)

        xn = rms_norm(xp, norm_mix[l])
        fq, fk, fv, logf, dq, dk, dv, mq, gates = split_projection(
            xn, w_in[l], b_f[l], fox_q_norm[l], fox_k_norm[l], diff_q_norm[l], diff_k_norm[l], mem_q_norm[l])
        mk, mv = memory_kv(mem_prompt, norm_mem[l], w_mem_kv[l], mem_k_norm[l])
        o_fox = fox_prompt(fq, fk, fv, logf)
        o_diff = diff_post(diff_prompt(dq, dk, dv, lam, slopes), diff_out_norm[l], lam_init)
        o_mem = memory_attention(mq, mk, mv)
        h = xp + merge_branches(gates, o_fox, o_diff, o_mem, w_proj_fox[l], w_proj_diff[l], w_proj_mem[l], w_out[l])
        xp = h + peer_ffn(rms_norm(h, norm_ffn[l]), w_peer_q[l], peer_subkeys[l], peer_u[l], peer_v[l])
        p_fk.append(fk); p_fv.append(fv); p_fl.append(logf); p_dk.append(dk); p_dv.append(dv)
        p_mk.append(mk); p_mv.append(mv)

        xn = rms_norm(xs, norm_mix[l])
        fq, fk, fv, logf, dq, dk, dv, mq, gates = split_projection(
            xn, w_in[l], b_f[l], fox_q_norm[l], fox_k_norm[l], diff_q_norm[l], diff_k_norm[l], mem_q_norm[l])
        o_fox = fox_sample(fq, fk, fv, logf, cache_fox_k, cache_fox_v, cache_fox_logf, l, page_table)
        o_diff = diff_post(diff_sample(dq, dk, dv, cache_diff_k, cache_diff_v, l, page_table, lam, slopes),
                           diff_out_norm[l], lam_init)
        o_mem = memory_attention(mq, cache_mem_k[l], cache_mem_v[l])
        h = xs + merge_branches(gates, o_fox, o_diff, o_mem, w_proj_fox[l], w_proj_diff[l], w_proj_mem[l], w_out[l])
        xs = h + peer_ffn(rms_norm(h, norm_ffn[l]), w_peer_q[l], peer_subkeys[l], peer_u[l], peer_v[l])
        s_fk.append(fk); s_fv.append(fv); s_fl.append(logf); s_dk.append(dk); s_dv.append(dv)

    return (xp, xs,
            jnp.stack(p_fk), jnp.stack(p_fv), jnp.stack(p_fl), jnp.stack(p_dk), jnp.stack(p_dv),
            jnp.stack(p_mk), jnp.stack(p_mv),
            jnp.stack(s_fk), jnp.stack(s_fv), jnp.stack(s_fl), jnp.stack(s_dk), jnp.stack(s_dv))
```

```python
import functools

import numpy as np
import jax
import jax.numpy as jnp
from jax import lax
from jax.experimental import pallas as pl
from jax.experimental.pallas import tpu as pltpu

F32 = jnp.float32
BF16 = jnp.bfloat16

HEAD_DIM = 64
FOX_HEADS = 8
DIFF_HEADS = 4
MEM_HEADS = 4
MEM_HEAD_DIM = 128
N_BRANCH = 3
PAGE_SIZE = 128
BRANCH_WIDTH = 512
SCORE_HEADS = 8
PEER_HEADS = 8
PEER_TOPK = 16
N_KEYS = 128
RMS_EPS = 1e-6
ATTN_SCALE = HEAD_DIM ** -0.5
MEM_SCALE = MEM_HEAD_DIM ** -0.5
LANES = 128
SUBLANES = 8
VMEM_LIMIT_BYTES = 56 * 1024 * 1024
NEG_INF = float("-inf")


def _cparams(*sem):
    return pltpu.CompilerParams(dimension_semantics=sem, vmem_limit_bytes=VMEM_LIMIT_BYTES)


def _const_spec(shape):
    nd = len(shape)
    return pl.BlockSpec(shape, lambda *_: (0,) * nd, pipeline_mode=pl.Buffered(1))


def _rms_scale(x):
    return x * lax.rsqrt(jnp.mean(x * x, axis=-1, keepdims=True) + RMS_EPS)


def _split3(x):
    hi = x.astype(BF16)
    r = x - hi.astype(F32)
    mid = r.astype(BF16)
    lo = (r - mid.astype(F32)).astype(BF16)
    return hi, mid, lo


def _dot(a, b):
    return jnp.dot(a, b, preferred_element_type=F32)


def _dot_nt(a, b):
    return lax.dot_general(a, b, (((1,), (1,)), ((), ())), preferred_element_type=F32)


def _proj_kernel(x_ref, nmix_ref, wmain_ref, wff_ref, bf_ref, bd64_ref,
                 gfq_ref, gfk_ref, gdq_ref, gdk_ref, gmq_ref,
                 pqf_ref, pkf_ref, pqd_ref, pkd_ref,
                 fk_ref, fv_ref, lf_ref, dk_ref, dv_ref, mq_ref, gates_ref,
                 qf_ref, kf_ref, qd_ref, kd_ref, vfb_ref, vdb_ref,
                 carry_ref, *, tt, d_model):
    i = pl.program_id(1)
    w = BRANCH_WIDTH
    x = x_ref[0]
    xn = (_rms_scale(x) * nmix_ref[...]).astype(BF16)

    def proj(c0, n):
        return _dot(xn, wmain_ref[:, c0:c0 + n])

    bd64 = bd64_ref[...]

    def headnorm64(z, g_ref):
        ms = _dot((z * z).astype(BF16), bd64)
        return z * lax.rsqrt(ms + RMS_EPS) * g_ref[...]

    fq = headnorm64(proj(0, w), gfq_ref)
    fk = headnorm64(proj(w, w), gfk_ref)
    fv = proj(2 * w, w)
    dq = headnorm64(proj(3 * w, w), gdq_ref)
    dk = headnorm64(proj(4 * w, w), gdk_ref)
    dv = proj(5 * w, w)
    zq = proj(6 * w, w)
    mq = jnp.concatenate(
        [_rms_scale(zq[:, MEM_HEAD_DIM * h:MEM_HEAD_DIM * (h + 1)]) * gmq_ref[...] for h in range(MEM_HEADS)], axis=1)
    gates = jax.nn.sigmoid(proj(7 * w, N_BRANCH * d_model))

    ff = _dot(xn, wff_ref[...]) + bf_ref[...]
    lf = jnp.minimum(ff, 0.0) - jnp.log1p(jnp.exp(-jnp.abs(ff)))

    @pl.when(i == 0)
    def _():
        carry_ref[...] = jnp.zeros_like(carry_ref)

    row = lax.broadcasted_iota(jnp.int32, (tt, tt), 0)
    col = lax.broadcasted_iota(jnp.int32, (tt, tt), 1)
    tri = jnp.where(col <= row, 1.0, 0.0).astype(BF16)
    c = _dot(tri, jnp.concatenate(_split3(lf), axis=1))
    fc = c[:, :LANES] + c[:, LANES:2 * LANES] + c[:, 2 * LANES:] + carry_ref[...]
    carry_ref[...] = fc[tt - 1:tt, :]

    lane = lax.broadcasted_iota(jnp.int32, (tt, LANES), 1)
    one_lane0 = jnp.where(lane == 0, 1.0, 0.0).astype(BF16)
    fcat = jnp.concatenate(_split3(fc) + (one_lane0,), axis=1)
    qb_fox = _dot(fcat, pqf_ref[...])
    kb_fox = _dot(fcat, pkf_ref[...])

    pos = i * tt + lax.broadcasted_iota(jnp.int32, (tt, LANES), 0)
    pos_hi = ((pos >> 7) << 7).astype(F32)
    pos_lo = (pos & 127).astype(F32)
    pcat = jnp.where(lane == 0, pos_hi, jnp.where(lane == 1, pos_lo, jnp.where(lane == 2, 1.0, 0.0))).astype(BF16)
    qb_diff = _dot(pcat, pqd_ref[...])
    kb_diff = _dot(pcat, pkd_ref[...])

    def pack(z, bias, out_ref, scale):
        for h in range(SCORE_HEADS):
            base = z[:, LANES * (h // 2):LANES * (h // 2 + 1)]
            if h % 2:
                base = pltpu.roll(base, HEAD_DIM, 1)
            if scale != 1.0:
                base = base * scale
            out_ref[0, h] = jnp.where(lane < HEAD_DIM, base, bias[:, LANES * h:LANES * (h + 1)]).astype(BF16)

    pack(fq, qb_fox, qf_ref, ATTN_SCALE)
    pack(fk, kb_fox, kf_ref, 1.0)
    pack(dq, qb_diff, qd_ref, ATTN_SCALE)
    pack(dk, kb_diff, kd_ref, 1.0)

    fk_ref[0] = fk
    fv_ref[0] = fv
    lf_ref[0] = lf[:, :FOX_HEADS]
    dk_ref[0] = dk
    dv_ref[0] = dv
    mq_ref[0] = mq
    gates_ref[0] = gates
    vfb_ref[0] = fv.astype(BF16)
    vdb_ref[0] = dv.astype(BF16)


def _placement_matrices():
    n = SCORE_HEADS * LANES
    pqf = np.zeros((4 * LANES, n), np.float32)
    pkf = np.zeros((4 * LANES, n), np.float32)
    pqd = np.zeros((LANES, n), np.float32)
    pkd = np.zeros((LANES, n), np.float32)
    for h in range(SCORE_HEADS):
        c = h * LANES + HEAD_DIM
        for part in range(3):
            pqf[part * LANES + h, c + part] = 1.0
            pqf[3 * LANES, c + 3 + part] = 1.0
            pkf[3 * LANES, c + part] = 1.0
            pkf[part * LANES + h, c + 3 + part] = -1.0
        slope = 2.0 ** (-8.0 * ((h % DIFF_HEADS) + 1) / DIFF_HEADS)
        pqd[0, c] = -slope
        pqd[1, c + 1] = -slope
        pqd[2, c + 2] = 1.0
        pqd[2, c + 3] = 1.0
        pkd[2, c] = 1.0
        pkd[2, c + 1] = 1.0
        pkd[0, c + 2] = slope
        pkd[1, c + 3] = slope
    return [jnp.asarray(m, BF16) for m in (pqf, pkf, pqd, pkd)]


def _proj(x, weights, tt):
    b, s, d = x.shape
    w = BRANCH_WIDTH
    nt = s // tt
    tok = lambda width: pl.BlockSpec((1, tt, width), lambda bi, i: (bi, i, 0))
    head = pl.BlockSpec((1, SCORE_HEADS, tt, LANES), lambda bi, i: (bi, 0, i, 0))
    consts = [weights[k] for k in ("nmix", "wmain", "wff", "bf", "bd64", "gfq", "gfk", "gdq", "gdk", "gmq",
                                   "pqf", "pkf", "pqd", "pkd")]
    names = ("fk", "fv", "logf", "dk", "dv", "mq", "gates", "qf", "kf", "qd", "kd", "vfb", "vdb")
    shapes = ([jax.ShapeDtypeStruct((b, s, w), F32)] * 2 + [jax.ShapeDtypeStruct((b, s, FOX_HEADS), F32)]
              + [jax.ShapeDtypeStruct((b, s, w), F32)] * 3 + [jax.ShapeDtypeStruct((b, s, N_BRANCH * d), F32)]
              + [jax.ShapeDtypeStruct((b, SCORE_HEADS, s, LANES), BF16)] * 4 + [jax.ShapeDtypeStruct((b, s, w), BF16)] * 2)
    specs = ([tok(w)] * 2 + [tok(FOX_HEADS)] + [tok(w)] * 3 + [tok(N_BRANCH * d)] + [head] * 4 + [tok(w)] * 2)
    outs = pl.pallas_call(
        functools.partial(_proj_kernel, tt=tt, d_model=d),
        grid=(b, nt),
        in_specs=[tok(d)] + [_const_spec(c.shape) for c in consts],
        out_specs=specs,
        out_shape=shapes,
        scratch_shapes=[pltpu.VMEM((1, LANES), F32)],
        compiler_params=_cparams("arbitrary", "arbitrary"),
        name="proj",
    )(x, *consts)
    return dict(zip(names, outs))


def _flash_kernel(qi_ref, kj_ref, q_ref, k_ref, v_ref, o_ref, m_ref, l_ref, acc_ref, *, group, tq):
    p = pl.program_id(2)
    qi = qi_ref[p]
    kj = kj_ref[p]

    @pl.when(kj == 0)
    def _():
        m_ref[...] = jnp.full_like(m_ref, NEG_INF)
        l_ref[...] = jnp.zeros_like(l_ref)
        acc_ref[...] = jnp.zeros_like(acc_ref)

    def step(diagonal):
        v = v_ref[0]
        for g in range(group):
            s = _dot_nt(q_ref[0, g], k_ref[0, g])
            if diagonal:
                row = lax.broadcasted_iota(jnp.int32, s.shape, 0)
                col = lax.broadcasted_iota(jnp.int32, s.shape, 1)
                s = jnp.where(col <= row, s, NEG_INF)
            m_prev = m_ref[g]
            m_new = jnp.maximum(m_prev, jnp.max(s, axis=1, keepdims=True))
            alpha = jnp.exp(m_prev - m_new)
            pexp = jnp.exp(s - m_new)
            l_ref[g] = alpha * l_ref[g] + jnp.sum(pexp, axis=1, keepdims=True)
            acc_ref[g] = alpha * acc_ref[g] + _dot(pexp.astype(BF16), v)
            m_ref[g] = m_new

    @pl.when(kj < qi)
    def _():
        step(False)

    @pl.when(kj == qi)
    def _():
        step(True)
        if group == 2:
            lane = lax.broadcasted_iota(jnp.int32, (tq, LANES), 1)
            o_ref[0] = jnp.where(lane < HEAD_DIM, acc_ref[0] / l_ref[0], acc_ref[1] / l_ref[1])
        else:
            o_ref[0] = acc_ref[0] / l_ref[0]


def _flash(q, k, v, *, group, tq, name):
    b, hs, s, _ = q.shape
    ng = hs // group
    nq = s // tq
    pairs = [(i, j) for i in range(nq) for j in range(i + 1)]
    qi_tab = jnp.asarray([p[0] for p in pairs], jnp.int32)
    kj_tab = jnp.asarray([p[1] for p in pairs], jnp.int32)
    v_blocks = v.shape[2] // LANES
    grid_spec = pltpu.PrefetchScalarGridSpec(
        num_scalar_prefetch=2,
        grid=(b, ng, len(pairs)),
        in_specs=[
            pl.BlockSpec((1, group, tq, LANES), lambda bi, g, p, qt, kt: (bi, g, qt[p], 0)),
            pl.BlockSpec((1, group, tq, LANES), lambda bi, g, p, qt, kt: (bi, g, kt[p], 0)),
            pl.BlockSpec((1, tq, LANES), lambda bi, g, p, qt, kt: (bi, kt[p], g % v_blocks)),
        ],
        out_specs=pl.BlockSpec((1, tq, LANES), lambda bi, g, p, qt, kt: (bi, qt[p], g)),
        scratch_shapes=[pltpu.VMEM((group, tq, 1), F32), pltpu.VMEM((group, tq, 1), F32),
                        pltpu.VMEM((group, tq, LANES), F32)],
    )
    return pl.pallas_call(
        functools.partial(_flash_kernel, group=group, tq=tq),
        grid_spec=grid_spec,
        out_shape=jax.ShapeDtypeStruct((b, s, ng * LANES), F32),
        compiler_params=_cparams("arbitrary", "arbitrary", "arbitrary"),
        name=name,
    )(qi_tab, kj_tab, q, k, v)


def _memkv_kernel(x_ref, nmem_ref, w_ref, gk_ref, mk_ref, mv_ref):
    xn = (_rms_scale(x_ref[...]) * nmem_ref[...]).astype(BF16)
    z = _dot(xn, w_ref[...])
    mk_ref[...] = jnp.concatenate(
        [_rms_scale(z[:, MEM_HEAD_DIM * h:MEM_HEAD_DIM * (h + 1)]) * gk_ref[...] for h in range(MEM_HEADS)], axis=1)
    mv_ref[...] = z[:, BRANCH_WIDTH:]


def _memkv(mem, nmem, w_kv, gk, tm):
    t, d = mem.shape
    w = BRANCH_WIDTH
    return pl.pallas_call(
        _memkv_kernel,
        grid=(t // tm,),
        in_specs=[pl.BlockSpec((tm, d), lambda i: (i, 0)), _const_spec(nmem.shape), _const_spec(w_kv.shape),
                  _const_spec(gk.shape)],
        out_specs=[pl.BlockSpec((tm, w), lambda i: (i, 0))] * 2,
        out_shape=[jax.ShapeDtypeStruct((t, w), F32)] * 2,
        compiler_params=_cparams("arbitrary"),
        name="memkv",
    )(mem, nmem, w_kv, gk)


def _memattn_kernel(q_ref, k_ref, v_ref, o_ref):
    outs = []
    for h in range(MEM_HEADS):
        sl = slice(MEM_HEAD_DIM * h, MEM_HEAD_DIM * (h + 1))
        q = q_ref[0][:, sl].astype(BF16)
        k = k_ref[0][:, sl].astype(BF16)
        v = v_ref[0][:, sl].astype(BF16)
        s = _dot_nt(q, k) * MEM_SCALE
        e = jnp.exp(s - jnp.max(s, axis=1, keepdims=True))
        outs.append(_dot(e.astype(BF16), v) / jnp.sum(e, axis=1, keepdims=True))
    o_ref[0] = jnp.concatenate(outs, axis=1)


def _memattn(q, k, v, tq):
    bm, tm, w = q.shape
    nm = k.shape[1]
    return pl.pallas_call(
        _memattn_kernel,
        grid=(bm, tm // tq),
        in_specs=[pl.BlockSpec((1, tq, w), lambda b, i: (b, i, 0)),
                  pl.BlockSpec((1, nm, w), lambda b, i: (b, 0, 0)),
                  pl.BlockSpec((1, nm, w), lambda b, i: (b, 0, 0))],
        out_specs=pl.BlockSpec((1, tq, w), lambda b, i: (b, i, 0)),
        out_shape=jax.ShapeDtypeStruct((bm, tm, w), F32),
        compiler_params=_cparams("arbitrary", "arbitrary"),
        name="memattn",
    )(q, k, v)


def _merge_kernel(x_ref, ofox_ref, od_ref, omem_ref, gates_ref, dlam_ref, gdo_ref,
                  wpf_ref, wpd_ref, wpm_ref, wout_ref, nffn_ref, wq_ref, subk_ref,
                  h_ref, xnt_ref, st_ref, *, d_model, lam_init):
    w = BRANCH_WIDTH
    dl = dlam_ref[...]
    lam = (jnp.exp(jnp.sum(dl[0:1] * dl[1:2], axis=1, keepdims=True))
           - jnp.exp(jnp.sum(dl[2:3] * dl[3:4], axis=1, keepdims=True)) + lam_init)
    od = od_ref[...]
    o = od[:, :w] - lam * od[:, w:]
    hw = 2 * HEAD_DIM
    odn = jnp.concatenate(
        [_rms_scale(o[:, hw * h:hw * (h + 1)]) * gdo_ref[...] * (1.0 - lam_init) for h in range(DIFF_HEADS)], axis=1)
    b_fox = _dot(ofox_ref[...].astype(BF16), wpf_ref[...])
    b_diff = _dot(odn.astype(BF16), wpd_ref[...])
    b_mem = _dot(omem_ref[...].astype(BF16), wpm_ref[...])
    gates = gates_ref[...]
    m = (gates[:, :d_model] * b_fox + gates[:, d_model:2 * d_model] * b_diff + gates[:, 2 * d_model:] * b_mem)
    h = x_ref[...] + _dot(m.astype(BF16), wout_ref[...])
    h_ref[...] = h
    xn = _rms_scale(h) * nffn_ref[...]
    xnb = xn.astype(BF16)
    xnt_ref[...] = xn.T.astype(BF16)
    q = _dot(xnb, wq_ref[...])
    for hc in range(2 * PEER_HEADS):
        qhc = q[:, N_KEYS * hc:N_KEYS * (hc + 1)].astype(BF16)
        st_ref[hc] = _dot_nt(subk_ref[hc % 2], qhc)


def _merge(x, ofox, od, omem, gates, weights, tt):
    t, d = x.shape
    w = BRANCH_WIDTH
    consts = [weights[k] for k in ("dlam", "gdo", "wpf", "wpd", "wpm", "wout", "nffn", "wq", "subk")]
    rows = lambda width: pl.BlockSpec((tt, width), lambda i: (i, 0))
    return pl.pallas_call(
        functools.partial(_merge_kernel, d_model=d, lam_init=weights["lam_init"]),
        grid=(t // tt,),
        in_specs=[rows(d), rows(w), rows(2 * w), rows(w), rows(N_BRANCH * d)] + [_const_spec(c.shape) for c in consts],
        out_specs=[rows(d), pl.BlockSpec((d, tt), lambda i: (0, i)),
                   pl.BlockSpec((2 * PEER_HEADS, N_KEYS, tt), lambda i: (0, 0, i))],
        out_shape=[jax.ShapeDtypeStruct((t, d), F32), jax.ShapeDtypeStruct((d, t), BF16),
                   jax.ShapeDtypeStruct((2 * PEER_HEADS, N_KEYS, t), F32)],
        compiler_params=_cparams("arbitrary"),
        name="merge",
    )(x, ofox, od, omem, gates, *consts)


def _oddeven_merge(lo, hi, r):
    step = r * 2
    if step < hi - lo:
        yield from _oddeven_merge(lo, hi, step)
        yield from _oddeven_merge(lo + r, hi, step)
        yield from [(i, i + r) for i in range(lo + r, hi - r, step)]
    else:
        yield (lo, lo + r)


def _oddeven_merge_sort(lo, hi):
    if hi - lo >= 1:
        mid = lo + (hi - lo) // 2
        yield from _oddeven_merge_sort(lo, mid)
        yield from _oddeven_merge_sort(mid + 1, hi)
        yield from _oddeven_merge(lo, hi, 1)


_SORT16 = tuple(_oddeven_merge_sort(0, PEER_TOPK - 1))


def _cmpx(v, i, j):
    a, b = v[i], v[j]
    if b is None:
        return
    if a is None:
        v[i], v[j] = b, None
        return
    v[i], v[j] = jnp.maximum(a, b), jnp.minimum(a, b)


def _top16_replicated(v):
    v = list(v)
    for i, j in _SORT16:
        _cmpx(v, i, j)
    for shift in (4, 2, 1):
        part = [None if a is None else pltpu.roll(a, shift, 0) for a in v]
        merged = []
        for i in range(PEER_TOPK):
            a, b = v[i], part[PEER_TOPK - 1 - i]
            merged.append(b if a is None else a if b is None else jnp.maximum(a, b))
        for stride in (8, 4, 2, 1):
            for i in range(PEER_TOPK):
                if not i & stride:
                    _cmpx(merged, i, i + stride)
        v = merged
    return v


def _topk_kernel(st_ref, e1_ref, e2_ref, tau_ref, *, tl):
    sub = lax.broadcasted_iota(jnp.int32, (SUBLANES, tl), 0)

    def spread(vals):
        out = vals[SUBLANES - 1]
        for s in range(SUBLANES - 2, -1, -1):
            out = jnp.where(sub == s, vals[s], out)
        return out

    def body(h, tau_acc):
        s1 = st_ref[2 * h]
        s2 = st_ref[2 * h + 1]
        a = _top16_replicated([s1[SUBLANES * i:SUBLANES * (i + 1)] for i in range(N_KEYS // SUBLANES)])
        b = _top16_replicated([s2[SUBLANES * i:SUBLANES * (i + 1)] for i in range(N_KEYS // SUBLANES)])
        b_lo, b_hi = spread(b[:SUBLANES]), spread(b[SUBLANES:])
        cand = [a[0] + b_lo, a[0] + b_hi] + [a[i] + b_lo for i in range(1, SUBLANES)] + [spread(a[SUBLANES:]) + b[0]]
        best = _top16_replicated(cand + [None] * (PEER_TOPK - len(cand)))
        z = jnp.exp(best[0] - best[0])
        for kk in range(1, PEER_TOPK):
            z = z + jnp.exp(best[kk] - best[0])
        reps = N_KEYS // SUBLANES
        e1_ref[h] = jnp.exp(s1 - jnp.concatenate([a[0]] * reps, axis=0))
        e2_ref[h] = jnp.exp(s2 - jnp.concatenate([b[0]] * reps, axis=0)) / jnp.concatenate([z] * reps, axis=0)
        return jnp.where(sub == h, best[PEER_TOPK - 1], tau_acc)

    tau_ref[...] = lax.fori_loop(0, PEER_HEADS, body, jnp.zeros((SUBLANES, tl), F32))


def _topk(st, tl):
    nhc, nk, t = st.shape
    blk = lambda n: pl.BlockSpec((n, nk, tl), lambda i: (0, 0, i))
    return pl.pallas_call(
        functools.partial(_topk_kernel, tl=tl),
        grid=(t // tl,),
        in_specs=[blk(nhc)],
        out_specs=[blk(PEER_HEADS), blk(PEER_HEADS), pl.BlockSpec((PEER_HEADS, tl), lambda i: (0, i))],
        out_shape=[jax.ShapeDtypeStruct((PEER_HEADS, nk, t), F32)] * 2 + [jax.ShapeDtypeStruct((PEER_HEADS, t), F32)],
        compiler_params=_cparams("arbitrary"),
        name="topk",
    )(st)


def _peer_kernel(xt_ref, u_ref, vt_ref, st_ref, e1_ref, e2_ref, tau_ref, h_ref, y_ref, acc_ref, *, eb, tt):
    ej = pl.program_id(1)

    @pl.when(ej == 0)
    def _():
        acc_ref[...] = jnp.zeros_like(acc_ref)

    ht = _dot(u_ref[...], xt_ref[...])
    acts = []
    for j in range(eb // N_KEYS):
        a = ej * (eb // N_KEYS) + j
        wt = jnp.zeros((N_KEYS, tt), F32)
        for h in range(PEER_HEADS):
            s = st_ref[2 * h, pl.ds(a, 1), :] + st_ref[2 * h + 1]
            val = e1_ref[h, pl.ds(a, 1), :] * e2_ref[h]
            wt = wt + jnp.where(s >= tau_ref[h:h + 1, :], val, 0.0)
        hj = ht[N_KEYS * j:N_KEYS * (j + 1)]
        gelu = 0.5 * hj * (1.0 + lax.erf(hj * (2.0 ** -0.5)))
        acts.append((wt * gelu).astype(BF16))
    acc_ref[...] += _dot(vt_ref[...], jnp.concatenate(acts, axis=0))

    @pl.when(ej == pl.num_programs(1) - 1)
    def _():
        y_ref[...] = h_ref[...] + acc_ref[...].T


def _peer(xt, u, vt, st, e1, e2, tau, h, tt, eb):
    d, t = xt.shape
    ne = u.shape[0]
    tok3 = lambda n: pl.BlockSpec((n, N_KEYS, tt), lambda i, j: (0, 0, i))
    return pl.pallas_call(
        functools.partial(_peer_kernel, eb=eb, tt=tt),
        grid=(t // tt, ne // eb),
        in_specs=[pl.BlockSpec((d, tt), lambda i, j: (0, i)),
                  pl.BlockSpec((eb, d), lambda i, j: (j, 0)),
                  pl.BlockSpec((d, eb), lambda i, j: (0, j)),
                  tok3(2 * PEER_HEADS), tok3(PEER_HEADS), tok3(PEER_HEADS),
                  pl.BlockSpec((PEER_HEADS, tt), lambda i, j: (0, i)),
                  pl.BlockSpec((tt, d), lambda i, j: (i, 0))],
        out_specs=pl.BlockSpec((tt, d), lambda i, j: (i, 0)),
        out_shape=jax.ShapeDtypeStruct((t, d), F32),
        scratch_shapes=[pltpu.VMEM((d, tt), F32)],
        compiler_params=_cparams("arbitrary", "arbitrary"),
        name="peer",
    )(xt, u, vt, st, e1, e2, tau, h)


def _paged_kernel(pt_ref, q_ref, knew_ref, vnew_ref, lg_ref, *rest, fox, n_pages, pages_per_step, dec_seq, past_len):
    npp = pages_per_step
    k_refs = rest[:npp]
    v_refs = rest[npp:2 * npp]
    rest = rest[2 * npp:]
    lf_refs = rest[:npp] if fox else ()
    o_ref, m_ref, l_ref, acc_ref, carry_ref = rest[-5:]
    j = pl.program_id(1)
    rows = dec_seq * SCORE_HEADS
    q = q_ref[0]

    def online(s, v):
        m_prev = m_ref[...]
        m_new = jnp.maximum(m_prev, jnp.max(s, axis=1, keepdims=True))
        alpha = jnp.exp(m_prev - m_new)
        pexp = jnp.exp(s - m_new)
        l_ref[...] = alpha * l_ref[...] + jnp.sum(pexp, axis=1, keepdims=True)
        acc_ref[...] = alpha * acc_ref[...] + _dot(pexp.astype(BF16), v)
        m_ref[...] = m_new

    hs = lax.broadcasted_iota(jnp.int32, (SCORE_HEADS, 1), 0) % DIFF_HEADS
    slope = jnp.where(hs == 0, 2.0 ** -2, jnp.where(hs == 1, 2.0 ** -4, jnp.where(hs == 2, 2.0 ** -6, 2.0 ** -8)))

    def new_gate(t):
        n = lg_ref[0][:, 0:1]
        for u in range(1, t + 1):
            n = n + lg_ref[0][:, u:u + 1]
        return n

    @pl.when(j == 0)
    def _():
        m_ref[...] = jnp.full_like(m_ref, NEG_INF)
        l_ref[...] = jnp.zeros_like(l_ref)
        acc_ref[...] = jnp.zeros_like(acc_ref)
        carry_ref[...] = jnp.zeros_like(carry_ref)
        nk = knew_ref.shape[1]
        s = _dot_nt(q, knew_ref[0])
        colh = lax.broadcasted_iota(jnp.int32, (SCORE_HEADS, nk), 1)
        parts = []
        for t in range(dec_seq):
            if fox:
                n_cols = jnp.zeros((SCORE_HEADS, nk), F32)
                for u in range(dec_seq):
                    n_cols = n_cols + jnp.where(colh >= u, lg_ref[0][:, u:u + 1], 0.0)
                bias = new_gate(t) - n_cols
            else:
                bias = -slope * (t - colh).astype(F32)
            st = s[SCORE_HEADS * t:SCORE_HEADS * (t + 1)] + bias
            parts.append(jnp.where(colh <= t, st, NEG_INF))
        online(jnp.concatenate(parts, axis=0), vnew_ref[0])

    @pl.when(j > 0)
    def _():
        tok = lax.broadcasted_iota(jnp.int32, (SCORE_HEADS, PAGE_SIZE), 1)
        if fox:
            ri = lax.broadcasted_iota(jnp.int32, (PAGE_SIZE, PAGE_SIZE), 0)
            ci = lax.broadcasted_iota(jnp.int32, (PAGE_SIZE, PAGE_SIZE), 1)
            later = jnp.where(ri > ci, 1.0, 0.0).astype(BF16)
        for i in range(npp):
            page = n_pages - 1 - ((j - 1) * npp + i)
            s = _dot_nt(q, k_refs[i][0].astype(BF16))
            if fox:
                lf = lf_refs[i][0]
                c = _dot(jnp.concatenate(_split3(lf), axis=0), later)
                suffix = c[:SCORE_HEADS] + c[SCORE_HEADS:2 * SCORE_HEADS] + c[2 * SCORE_HEADS:]
                after = suffix + carry_ref[...]
                carry_ref[...] = after[:, 0:1] + lf[:, 0:1]
                bias = [after + new_gate(t) for t in range(dec_seq)]
            else:
                kpos = page * PAGE_SIZE + tok
                bias = [-slope * (past_len + t - kpos).astype(F32) for t in range(dec_seq)]
            online(s + jnp.concatenate(bias, axis=0), v_refs[i][0].astype(BF16))

    @pl.when(j == pl.num_programs(1) - 1)
    def _():
        o_ref[0] = acc_ref[...] / l_ref[...]


def _paged(page_table, q, knew, vnew, lg, k_pool, v_pool, lf_pool, *, fox, pages_per_step, dec_seq):
    b, n_pages = page_table.shape
    npp = pages_per_step
    rows = q.shape[1]
    w = BRANCH_WIDTH
    pt = page_table.reshape(-1)

    def page_map(i):
        def index(bi, j, pt_ref):
            slot = n_pages - 1 - (jnp.maximum(j, 1) - 1) * npp - i
            return (pt_ref[bi * n_pages + slot], 0, 0)
        return index

    per_seq = lambda shape: pl.BlockSpec((1,) + shape, lambda bi, j, pt_ref: (bi, 0, 0))
    in_specs = [per_seq((rows, w)), per_seq(knew.shape[1:]), per_seq(vnew.shape[1:]), per_seq(lg.shape[1:])]
    in_specs += [pl.BlockSpec((1, PAGE_SIZE, w), page_map(i)) for i in range(npp)] * 2
    args = [q, knew, vnew, lg] + [k_pool] * npp + [v_pool] * npp
    if fox:
        in_specs += [pl.BlockSpec((1, FOX_HEADS, PAGE_SIZE), page_map(i)) for i in range(npp)]
        args += [lf_pool] * npp
    grid_spec = pltpu.PrefetchScalarGridSpec(
        num_scalar_prefetch=1,
        grid=(b, n_pages // npp + 1),
        in_specs=in_specs,
        out_specs=pl.BlockSpec((1, rows, w), lambda bi, j, pt_ref: (bi, 0, 0)),
        scratch_shapes=[pltpu.VMEM((rows, 1), F32), pltpu.VMEM((rows, 1), F32), pltpu.VMEM((rows, w), F32),
                        pltpu.VMEM((SCORE_HEADS, 1), F32)],
    )
    return pl.pallas_call(
        functools.partial(_paged_kernel, fox=fox, n_pages=n_pages, pages_per_step=npp, dec_seq=dec_seq,
                          past_len=n_pages * PAGE_SIZE),
        grid_spec=grid_spec,
        out_shape=jax.ShapeDtypeStruct((b, rows, w), F32),
        compiler_params=_cparams("arbitrary", "arbitrary"),
        name="paged_fox" if fox else "paged_diff",
    )(pt, *args)


def _tile(n, target):
    t = min(n, target)
    while n % t:
        t -= 1
    return t


def _layer_weights(layer, d, norm_mix, norm_mem, w_in, b_f, fox_q_norm, fox_k_norm, diff_q_norm, diff_k_norm,
                   diff_lambda, diff_out_norm, w_mem_kv, mem_q_norm, mem_k_norm, w_proj_fox, w_proj_diff,
                   w_proj_mem, w_out, norm_ffn, w_peer_q, peer_subkeys, peer_u, peer_v):
    w = BRANCH_WIDTH
    l = layer
    win = w_in[l]
    ff0 = 3 * w
    pad_lanes = lambda a: jnp.pad(a, ((0, 0), (0, LANES - a.shape[1])))
    pqf, pkf, pqd, pkd = _placement_matrices()
    bd64 = jnp.asarray(np.kron(np.eye(SCORE_HEADS), np.full((HEAD_DIM, HEAD_DIM), 1.0 / HEAD_DIM)), BF16)
    tile8 = lambda g: jnp.tile(g[l], SCORE_HEADS)[None, :]
    return dict(
        nmix=norm_mix[l][None, :],
        wmain=jnp.concatenate([win[:, :ff0], win[:, ff0 + FOX_HEADS:]], axis=1).astype(BF16),
        wff=pad_lanes(win[:, ff0:ff0 + FOX_HEADS]).astype(BF16),
        bf=pad_lanes(b_f[l][None, :]),
        bd64=bd64,
        gfq=tile8(fox_q_norm), gfk=tile8(fox_k_norm), gdq=tile8(diff_q_norm), gdk=tile8(diff_k_norm),
        gmq=mem_q_norm[l][None, :],
        pqf=pqf, pkf=pkf, pqd=pqd, pkd=pkd,
        nmem=norm_mem[l][None, :], wmemkv=w_mem_kv[l].astype(BF16), gmk=mem_k_norm[l][None, :],
        dlam=diff_lambda[l], gdo=diff_out_norm[l][None, :],
        wpf=w_proj_fox[l].astype(BF16), wpd=w_proj_diff[l].astype(BF16), wpm=w_proj_mem[l].astype(BF16),
        wout=w_out[l].astype(BF16), nffn=norm_ffn[l][None, :], wq=w_peer_q[l].astype(BF16),
        subk=peer_subkeys[l].astype(BF16),
        u=peer_u[l].astype(BF16), vt=peer_v[l].T.astype(BF16),
        lam_init=0.8 - 0.6 * float(np.exp(-0.3 * l)),
    )


def _ffn_tail(x, ofox, od, omem, gates, wts):
    t, d = x.shape
    h, xnt, st = _merge(x, ofox, od, omem, gates, wts, _tile(t, 256))
    e1, e2, tau = _topk(st, _tile(t, 256))
    return _peer(xnt, wts["u"], wts["vt"], st, e1, e2, tau, h, _tile(t, 512), _tile(wts["u"].shape[0], 1024))


def kernel(x_prompt, x_sample, mem_prompt, cache_fox_k, cache_fox_v, cache_fox_logf, cache_diff_k, cache_diff_v, cache_mem_k, cache_mem_v, page_table, norm_mix, norm_mem, w_in, b_f, fox_q_norm, fox_k_norm, diff_q_norm, diff_k_norm, diff_lambda, diff_out_norm, w_mem_kv, mem_q_norm, mem_k_norm, w_proj_fox, w_proj_diff, w_proj_mem, w_out, norm_ffn, w_peer_q, peer_subkeys, peer_u, peer_v):
    depth = w_in.shape[0]
    b, s, d = x_prompt.shape
    db, ds, _ = x_sample.shape
    n_mem = mem_prompt.shape[1]
    n_pool = cache_fox_k.shape[1]
    w = BRANCH_WIDTH
    xp, xs = x_prompt, x_sample
    outs = [[] for _ in range(12)]
    for l in range(depth):
        wts = _layer_weights(l, d, norm_mix, norm_mem, w_in, b_f, fox_q_norm, fox_k_norm, diff_q_norm, diff_k_norm,
                             diff_lambda, diff_out_norm, w_mem_kv, mem_q_norm, mem_k_norm, w_proj_fox, w_proj_diff,
                             w_proj_mem, w_out, norm_ffn, w_peer_q, peer_subkeys, peer_u, peer_v)

        pr = _proj(xp, wts, _tile(s, 256))
        mk, mv = _memkv(mem_prompt.reshape(b * n_mem, d), wts["nmem"], wts["wmemkv"], wts["gmk"], _tile(b * n_mem, 256))
        tq = _tile(s, 512)
        ofox = _flash(pr["qf"], pr["kf"], pr["vfb"], group=2, tq=tq, name="flash_fox")
        od = _flash(pr["qd"], pr["kd"], pr["vdb"], group=1, tq=tq, name="flash_diff")
        omem = _memattn(pr["mq"], mk.reshape(b, n_mem, w), mv.reshape(b, n_mem, w), _tile(s, 512))
        xp = _ffn_tail(xp.reshape(b * s, d), ofox.reshape(b * s, w), od.reshape(b * s, 2 * w), omem.reshape(b * s, w),
                       pr["gates"].reshape(b * s, N_BRANCH * d), wts).reshape(b, s, d)
        outs[0].append(pr["fk"].reshape(b, s, FOX_HEADS, HEAD_DIM))
        outs[1].append(pr["fv"].reshape(b, s, FOX_HEADS, HEAD_DIM))
        outs[2].append(pr["logf"])
        outs[3].append(pr["dk"].reshape(b, s, 2 * DIFF_HEADS, HEAD_DIM))
        outs[4].append(pr["dv"].reshape(b, s, DIFF_HEADS, 2 * HEAD_DIM))
        outs[5].append(mk.reshape(b, n_mem, MEM_HEADS, MEM_HEAD_DIM))
        outs[6].append(mv.reshape(b, n_mem, MEM_HEADS, MEM_HEAD_DIM))

        ts = db * ds
        sr = _proj(xs.reshape(1, ts, d), wts, _tile(ts, 256))
        pad_new = SUBLANES - ds

        def block_diag_q(qp):
            qh = qp[0, :, :, :HEAD_DIM].reshape(SCORE_HEADS, db, ds, HEAD_DIM).transpose(1, 2, 0, 3)
            eye = jnp.eye(SCORE_HEADS, dtype=qh.dtype)
            return (qh[:, :, :, None, :] * eye[None, None, :, :, None]).reshape(db, ds * SCORE_HEADS, w)

        def new_rows(a):
            return jnp.pad(a.reshape(db, ds, w), ((0, 0), (0, pad_new), (0, 0))).astype(BF16)

        lg = sr["logf"].reshape(db, ds, FOX_HEADS).transpose(0, 2, 1)
        lf_pool = cache_fox_logf[l].transpose(0, 2, 1)
        npp = _tile(page_table.shape[1], 8)
        ofs = _paged(page_table, block_diag_q(sr["qf"]), new_rows(sr["fk"][0]), new_rows(sr["fv"][0]), lg,
                     cache_fox_k[l].reshape(n_pool, PAGE_SIZE, w), cache_fox_v[l].reshape(n_pool, PAGE_SIZE, w),
                     lf_pool, fox=True, pages_per_step=npp, dec_seq=ds)
        ods = _paged(page_table, block_diag_q(sr["qd"]), new_rows(sr["dk"][0]), new_rows(sr["dv"][0]), lg,
                     cache_diff_k[l].reshape(n_pool, PAGE_SIZE, w), cache_diff_v[l].reshape(n_pool, PAGE_SIZE, w),
                     None, fox=False, pages_per_step=npp, dec_seq=ds)
        hidx = jnp.arange(FOX_HEADS)
        ofs = ofs.reshape(db, ds, FOX_HEADS, FOX_HEADS, HEAD_DIM)[:, :, hidx, hidx, :].reshape(ts, w)
        didx = jnp.arange(DIFF_HEADS)
        ods = ods.reshape(db, ds, 2, DIFF_HEADS, DIFF_HEADS, 2 * HEAD_DIM)[:, :, :, didx, didx, :].reshape(ts, 2 * w)
        mqs = jnp.pad(sr["mq"].reshape(db, ds, w), ((0, 0), (0, pad_new), (0, 0)))
        oms = _memattn(mqs, cache_mem_k[l].reshape(db, n_mem, w), cache_mem_v[l].reshape(db, n_mem, w), SUBLANES)
        oms = oms[:, :ds].reshape(ts, w)
        xs = _ffn_tail(xs.reshape(ts, d), ofs, ods, oms, sr["gates"].reshape(ts, N_BRANCH * d), wts).reshape(db, ds, d)
        outs[7].append(sr["fk"].reshape(db, ds, FOX_HEADS, HEAD_DIM))
        outs[8].append(sr["fv"].reshape(db, ds, FOX_HEADS, HEAD_DIM))
        outs[9].append(sr["logf"].reshape(db, ds, FOX_HEADS))
        outs[10].append(sr["dk"].reshape(db, ds, 2 * DIFF_HEADS, HEAD_DIM))
        outs[11].append(sr["dv"].reshape(db, ds, DIFF_HEADS, 2 * HEAD_DIM))

    return (xp, xs) + tuple(jnp.stack(o) for o in outs)
```

```python
import functools

import numpy as np
import jax
import jax.numpy as jnp
from jax import lax
from jax.experimental import pallas as pl
from jax.experimental.pallas import tpu as pltpu

F32 = jnp.float32
BF16 = jnp.bfloat16

HEAD_DIM = 64
FOX_HEADS = 8
DIFF_HEADS = 4
MEM_HEADS = 4
MEM_HEAD_DIM = 128
N_BRANCH = 3
PAGE_SIZE = 128
BRANCH_WIDTH = 512
SCORE_HEADS = 8
PEER_HEADS = 8
PEER_TOPK = 16
N_KEYS = 128
RMS_EPS = 1e-6
ATTN_SCALE = HEAD_DIM ** -0.5
MEM_SCALE = MEM_HEAD_DIM ** -0.5
LANES = 128
SUBLANES = 8
VMEM_LIMIT_BYTES = 56 * 1024 * 1024
FLASH_TQ = 512
FLASH_TK = 1024
NEG_INF = float("-inf")


def _cparams(*sem):
    return pltpu.CompilerParams(dimension_semantics=sem, vmem_limit_bytes=VMEM_LIMIT_BYTES)


def _const_spec(shape):
    nd = len(shape)
    return pl.BlockSpec(shape, lambda *_: (0,) * nd, pipeline_mode=pl.Buffered(1))


def _rms_scale(x):
    return x * lax.rsqrt(jnp.mean(x * x, axis=-1, keepdims=True) + RMS_EPS)


def _split3(x):
    hi = x.astype(BF16)
    r = x - hi.astype(F32)
    mid = r.astype(BF16)
    lo = (r - mid.astype(F32)).astype(BF16)
    return hi, mid, lo


def _dot(a, b):
    return jnp.dot(a, b, preferred_element_type=F32)


def _dot_nt(a, b):
    return lax.dot_general(a, b, (((1,), (1,)), ((), ())), preferred_element_type=F32)


def _proj_kernel(x_ref, nmix_ref, wmain_ref, wff_ref, bf_ref, bd64_ref,
                 gfq_ref, gfk_ref, gdq_ref, gdk_ref, gmq_ref,
                 pqf_ref, pkf_ref, pqd_ref, pkd_ref,
                 fk_ref, fv_ref, lf_ref, dk_ref, dv_ref, mq_ref, gates_ref,
                 qf_ref, kf_ref, qd_ref, kd_ref, vfb_ref, vdb_ref,
                 carry_ref, *, tt, d_model):
    i = pl.program_id(1)
    w = BRANCH_WIDTH
    x = x_ref[0]
    xn = (_rms_scale(x) * nmix_ref[...]).astype(BF16)

    def proj(c0, n):
        return _dot(xn, wmain_ref[:, c0:c0 + n])

    bd64 = bd64_ref[...]

    def headnorm64(z, g_ref):
        ms = _dot((z * z).astype(BF16), bd64)
        return z * lax.rsqrt(ms + RMS_EPS) * g_ref[...]

    fq = headnorm64(proj(0, w), gfq_ref)
    fk = headnorm64(proj(w, w), gfk_ref)
    fv = proj(2 * w, w)
    dq = headnorm64(proj(3 * w, w), gdq_ref)
    dk = headnorm64(proj(4 * w, w), gdk_ref)
    dv = proj(5 * w, w)
    zq = proj(6 * w, w)
    mq = jnp.concatenate(
        [_rms_scale(zq[:, MEM_HEAD_DIM * h:MEM_HEAD_DIM * (h + 1)]) * gmq_ref[...] for h in range(MEM_HEADS)], axis=1)
    gates = jax.nn.sigmoid(proj(7 * w, N_BRANCH * d_model))

    ff = _dot(xn, wff_ref[...]) + bf_ref[...]
    lf = jnp.minimum(ff, 0.0) - jnp.log1p(jnp.exp(-jnp.abs(ff)))

    @pl.when(i == 0)
    def _():
        carry_ref[...] = jnp.zeros_like(carry_ref)

    row = lax.broadcasted_iota(jnp.int32, (tt, tt), 0)
    col = lax.broadcasted_iota(jnp.int32, (tt, tt), 1)
    tri = jnp.where(col <= row, 1.0, 0.0).astype(BF16)
    c = _dot(tri, jnp.concatenate(_split3(lf), axis=1))
    fc = c[:, :LANES] + c[:, LANES:2 * LANES] + c[:, 2 * LANES:] + carry_ref[...]
    carry_ref[...] = fc[tt - 1:tt, :]

    lane = lax.broadcasted_iota(jnp.int32, (tt, LANES), 1)
    one_lane0 = jnp.where(lane == 0, 1.0, 0.0).astype(BF16)
    fcat = jnp.concatenate(_split3(fc) + (one_lane0,), axis=1)
    qb_fox = _dot(fcat, pqf_ref[...])
    kb_fox = _dot(fcat, pkf_ref[...])

    pos = i * tt + lax.broadcasted_iota(jnp.int32, (tt, LANES), 0)
    pos_hi = ((pos >> 7) << 7).astype(F32)
    pos_lo = (pos & 127).astype(F32)
    pcat = jnp.where(lane == 0, pos_hi, jnp.where(lane == 1, pos_lo, jnp.where(lane == 2, 1.0, 0.0))).astype(BF16)
    qb_diff = _dot(pcat, pqd_ref[...])
    kb_diff = _dot(pcat, pkd_ref[...])

    def pack(z, bias, out_ref, scale):
        for h in range(SCORE_HEADS):
            base = z[:, LANES * (h // 2):LANES * (h // 2 + 1)]
            if h % 2:
                base = pltpu.roll(base, HEAD_DIM, 1)
            if scale != 1.0:
                base = base * scale
            out_ref[0, h] = jnp.where(lane < HEAD_DIM, base, bias[:, LANES * h:LANES * (h + 1)]).astype(BF16)

    pack(fq, qb_fox, qf_ref, ATTN_SCALE)
    pack(fk, kb_fox, kf_ref, 1.0)
    pack(dq, qb_diff, qd_ref, ATTN_SCALE)
    pack(dk, kb_diff, kd_ref, 1.0)

    fk_ref[0] = fk
    fv_ref[0] = fv
    lf_ref[0] = lf[:, :FOX_HEADS]
    dk_ref[0] = dk
    dv_ref[0] = dv
    mq_ref[0] = mq
    gates_ref[0] = gates
    vfb_ref[0] = fv.T.astype(BF16)
    vdb_ref[0] = dv.T.astype(BF16)


def _placement_matrices():
    n = SCORE_HEADS * LANES
    pqf = np.zeros((4 * LANES, n), np.float32)
    pkf = np.zeros((4 * LANES, n), np.float32)
    pqd = np.zeros((LANES, n), np.float32)
    pkd = np.zeros((LANES, n), np.float32)
    for h in range(SCORE_HEADS):
        c = h * LANES + HEAD_DIM
        for part in range(3):
            pqf[part * LANES + h, c + part] = 1.0
            pqf[3 * LANES, c + 3 + part] = 1.0
            pkf[3 * LANES, c + part] = 1.0
            pkf[part * LANES + h, c + 3 + part] = -1.0
        slope = 2.0 ** (-8.0 * ((h % DIFF_HEADS) + 1) / DIFF_HEADS)
        pqd[0, c] = -slope
        pqd[1, c + 1] = -slope
        pqd[2, c + 2] = 1.0
        pqd[2, c + 3] = 1.0
        pkd[2, c] = 1.0
        pkd[2, c + 1] = 1.0
        pkd[0, c + 2] = slope
        pkd[1, c + 3] = slope
    return [jnp.asarray(m, BF16) for m in (pqf, pkf, pqd, pkd)]


def _proj(x, weights, tt):
    b, s, d = x.shape
    w = BRANCH_WIDTH
    nt = s // tt
    tok = lambda width: pl.BlockSpec((1, tt, width), lambda bi, i: (bi, i, 0))
    head = pl.BlockSpec((1, SCORE_HEADS, tt, LANES), lambda bi, i: (bi, 0, i, 0))
    consts = [weights[k] for k in ("nmix", "wmain", "wff", "bf", "bd64", "gfq", "gfk", "gdq", "gdk", "gmq",
                                   "pqf", "pkf", "pqd", "pkd")]
    names = ("fk", "fv", "logf", "dk", "dv", "mq", "gates", "qf", "kf", "qd", "kd", "vfb", "vdb")
    shapes = ([jax.ShapeDtypeStruct((b, s, w), F32)] * 2 + [jax.ShapeDtypeStruct((b, s, FOX_HEADS), F32)]
              + [jax.ShapeDtypeStruct((b, s, w), F32)] * 3 + [jax.ShapeDtypeStruct((b, s, N_BRANCH * d), F32)]
              + [jax.ShapeDtypeStruct((b, SCORE_HEADS, s, LANES), BF16)] * 4 + [jax.ShapeDtypeStruct((b, w, s), BF16)] * 2)
    specs = ([tok(w)] * 2 + [tok(FOX_HEADS)] + [tok(w)] * 3 + [tok(N_BRANCH * d)] + [head] * 4
             + [pl.BlockSpec((1, w, tt), lambda bi, i: (bi, 0, i))] * 2)
    outs = pl.pallas_call(
        functools.partial(_proj_kernel, tt=tt, d_model=d),
        grid=(b, nt),
        in_specs=[tok(d)] + [_const_spec(c.shape) for c in consts],
        out_specs=specs,
        out_shape=shapes,
        scratch_shapes=[pltpu.VMEM((1, LANES), F32)],
        compiler_params=_cparams("arbitrary", "arbitrary"),
        name="proj",
    )(x, *consts)
    return dict(zip(names, outs))


def _flash_kernel(qi_ref, kj_ref, fl_ref, q_ref, k_ref, vt_ref, o_ref, m_ref, l_ref, acc_ref, *, group, tq, tk, dv):
    p = pl.program_id(2)
    qi = qi_ref[p]
    kj = kj_ref[p]
    flags = fl_ref[p]

    @pl.when(kj == 0)
    def _():
        m_ref[...] = jnp.full_like(m_ref, NEG_INF)
        l_ref[...] = jnp.zeros_like(l_ref)
        acc_ref[...] = jnp.zeros_like(acc_ref)

    def step(masked):
        for g in range(group):
            st = _dot_nt(k_ref[0, g], q_ref[0, g])
            if masked:
                key = kj * tk + lax.broadcasted_iota(jnp.int32, st.shape, 0)
                qry = qi * tq + lax.broadcasted_iota(jnp.int32, st.shape, 1)
                st = jnp.where(key <= qry, st, NEG_INF)
            m_prev = m_ref[g]
            m_new = jnp.maximum(m_prev, jnp.max(st, axis=0, keepdims=True))
            alpha = jnp.exp(m_prev - m_new)
            pt = jnp.exp(st - m_new)
            l_ref[g] = alpha * l_ref[g] + jnp.sum(pt, axis=0, keepdims=True)
            acc_ref[g] = alpha * acc_ref[g] + _dot(vt_ref[0, g * dv:(g + 1) * dv, :], pt.astype(BF16))
            m_ref[g] = m_new

    @pl.when((flags & 1) == 0)
    def _():
        step(False)

    @pl.when((flags & 1) == 1)
    def _():
        step(True)

    @pl.when((flags & 2) == 2)
    def _():
        for g in range(group):
            o_ref[0, g * dv:(g + 1) * dv, :] = acc_ref[g] / l_ref[g]


def _flash(q, k, vt, *, group, dv, tq, tk, name):
    b, hs, s, _ = q.shape
    ng = hs // group
    v_blocks = vt.shape[1] // (group * dv)
    pairs = [(i, j) for i in range(s // tq) for j in range((i * tq + tq - 1) // tk + 1)]
    qi_tab = jnp.asarray([i for i, _ in pairs], jnp.int32)
    kj_tab = jnp.asarray([j for _, j in pairs], jnp.int32)
    fl_tab = jnp.asarray([((j + 1) * tk - 1 > i * tq) + 2 * (j == (i * tq + tq - 1) // tk) for i, j in pairs], jnp.int32)
    grid_spec = pltpu.PrefetchScalarGridSpec(
        num_scalar_prefetch=3,
        grid=(b, ng, len(pairs)),
        in_specs=[
            pl.BlockSpec((1, group, tq, LANES), lambda bi, g, p, qt, kt, ft: (bi, g, qt[p], 0)),
            pl.BlockSpec((1, group, tk, LANES), lambda bi, g, p, qt, kt, ft: (bi, g, kt[p], 0)),
            pl.BlockSpec((1, group * dv, tk), lambda bi, g, p, qt, kt, ft: (bi, g % v_blocks, kt[p])),
        ],
        out_specs=pl.BlockSpec((1, group * dv, tq), lambda bi, g, p, qt, kt, ft: (bi, g, qt[p])),
        scratch_shapes=[pltpu.VMEM((group, 1, tq), F32), pltpu.VMEM((group, 1, tq), F32),
                        pltpu.VMEM((group, dv, tq), F32)],
    )
    return pl.pallas_call(
        functools.partial(_flash_kernel, group=group, tq=tq, tk=tk, dv=dv),
        grid_spec=grid_spec,
        out_shape=jax.ShapeDtypeStruct((b, hs * dv, s), F32),
        compiler_params=_cparams("arbitrary", "arbitrary", "arbitrary"),
        name=name,
    )(qi_tab, kj_tab, fl_tab, q, k, vt)


def _memkv_kernel(x_ref, nmem_ref, w_ref, gk_ref, mk_ref, mv_ref):
    xn = (_rms_scale(x_ref[...]) * nmem_ref[...]).astype(BF16)
    z = _dot(xn, w_ref[...])
    mk_ref[...] = jnp.concatenate(
        [_rms_scale(z[:, MEM_HEAD_DIM * h:MEM_HEAD_DIM * (h + 1)]) * gk_ref[...] for h in range(MEM_HEADS)], axis=1)
    mv_ref[...] = z[:, BRANCH_WIDTH:]


def _memkv(mem, nmem, w_kv, gk, tm):
    t, d = mem.shape
    w = BRANCH_WIDTH
    return pl.pallas_call(
        _memkv_kernel,
        grid=(t // tm,),
        in_specs=[pl.BlockSpec((tm, d), lambda i: (i, 0)), _const_spec(nmem.shape), _const_spec(w_kv.shape),
                  _const_spec(gk.shape)],
        out_specs=[pl.BlockSpec((tm, w), lambda i: (i, 0))] * 2,
        out_shape=[jax.ShapeDtypeStruct((t, w), F32)] * 2,
        compiler_params=_cparams("arbitrary"),
        name="memkv",
    )(mem, nmem, w_kv, gk)


def _memattn_kernel(q_ref, k_ref, v_ref, o_ref):
    outs = []
    for h in range(MEM_HEADS):
        sl = slice(MEM_HEAD_DIM * h, MEM_HEAD_DIM * (h + 1))
        q = q_ref[0][:, sl].astype(BF16)
        k = k_ref[0][:, sl].astype(BF16)
        v = v_ref[0][:, sl].astype(BF16)
        s = _dot_nt(q, k) * MEM_SCALE
        e = jnp.exp(s - jnp.max(s, axis=1, keepdims=True))
        outs.append(_dot(e.astype(BF16), v) / jnp.sum(e, axis=1, keepdims=True))
    o_ref[0] = jnp.concatenate(outs, axis=1)


def _memattn(q, k, v, tq):
    bm, tm, w = q.shape
    nm = k.shape[1]
    return pl.pallas_call(
        _memattn_kernel,
        grid=(bm, tm // tq),
        in_specs=[pl.BlockSpec((1, tq, w), lambda b, i: (b, i, 0)),
                  pl.BlockSpec((1, nm, w), lambda b, i: (b, 0, 0)),
                  pl.BlockSpec((1, nm, w), lambda b, i: (b, 0, 0))],
        out_specs=pl.BlockSpec((1, tq, w), lambda b, i: (b, i, 0)),
        out_shape=jax.ShapeDtypeStruct((bm, tm, w), F32),
        compiler_params=_cparams("arbitrary", "arbitrary"),
        name="memattn",
    )(q, k, v)


def _merge_kernel(x_ref, ofox_ref, od_ref, omem_ref, gates_ref, dlam_ref, gdo_ref,
                  wpf_ref, wpd_ref, wpm_ref, wout_ref, nffn_ref, wq_ref, subk_ref,
                  h_ref, xnt_ref, st_ref, *, d_model, lam_init, attn_t):
    w = BRANCH_WIDTH
    dl = dlam_ref[...]
    lam = (jnp.exp(jnp.sum(dl[0:1] * dl[1:2], axis=1, keepdims=True))
           - jnp.exp(jnp.sum(dl[2:3] * dl[3:4], axis=1, keepdims=True)) + lam_init)
    if attn_t:
        od = od_ref[0].T
        ofox = ofox_ref[0].T
    else:
        od = od_ref[...]
        ofox = ofox_ref[...]
    o = od[:, :w] - lam * od[:, w:]
    hw = 2 * HEAD_DIM
    odn = jnp.concatenate(
        [_rms_scale(o[:, hw * h:hw * (h + 1)]) * gdo_ref[...] * (1.0 - lam_init) for h in range(DIFF_HEADS)], axis=1)
    b_fox = _dot(ofox.astype(BF16), wpf_ref[...])
    b_diff = _dot(odn.astype(BF16), wpd_ref[...])
    b_mem = _dot(omem_ref[...].astype(BF16), wpm_ref[...])
    gates = gates_ref[...]
    m = (gates[:, :d_model] * b_fox + gates[:, d_model:2 * d_model] * b_diff + gates[:, 2 * d_model:] * b_mem)
    h = x_ref[...] + _dot(m.astype(BF16), wout_ref[...])
    h_ref[...] = h
    xn = _rms_scale(h) * nffn_ref[...]
    xnb = xn.astype(BF16)
    xnt_ref[...] = xn.T.astype(BF16)
    q = _dot(xnb, wq_ref[...])
    for hc in range(2 * PEER_HEADS):
        qhc = q[:, N_KEYS * hc:N_KEYS * (hc + 1)].astype(BF16)
        st_ref[hc] = _dot_nt(subk_ref[hc % 2], qhc)


def _merge(x, ofox, od, omem, gates, weights, tt):
    t, d = x.shape
    w = BRANCH_WIDTH
    consts = [weights[k] for k in ("dlam", "gdo", "wpf", "wpd", "wpm", "wout", "nffn", "wq", "subk")]
    rows = lambda width: pl.BlockSpec((tt, width), lambda i: (i, 0))
    attn_t = ofox.ndim == 3
    if attn_t:
        nts = ofox.shape[2] // tt
        attn = lambda width: pl.BlockSpec((1, width, tt), lambda i: (i // nts, 0, i % nts))
    else:
        attn = rows
    return pl.pallas_call(
        functools.partial(_merge_kernel, d_model=d, lam_init=weights["lam_init"], attn_t=attn_t),
        grid=(t // tt,),
        in_specs=[rows(d), attn(w), attn(2 * w), rows(w), rows(N_BRANCH * d)] + [_const_spec(c.shape) for c in consts],
        out_specs=[rows(d), pl.BlockSpec((d, tt), lambda i: (0, i)),
                   pl.BlockSpec((2 * PEER_HEADS, N_KEYS, tt), lambda i: (0, 0, i))],
        out_shape=[jax.ShapeDtypeStruct((t, d), F32), jax.ShapeDtypeStruct((d, t), BF16),
                   jax.ShapeDtypeStruct((2 * PEER_HEADS, N_KEYS, t), F32)],
        compiler_params=_cparams("arbitrary"),
        name="merge",
    )(x, ofox, od, omem, gates, *consts)


def _oddeven_merge(lo, hi, r):
    step = r * 2
    if step < hi - lo:
        yield from _oddeven_merge(lo, hi, step)
        yield from _oddeven_merge(lo + r, hi, step)
        yield from [(i, i + r) for i in range(lo + r, hi - r, step)]
    else:
        yield (lo, lo + r)


def _oddeven_merge_sort(lo, hi):
    if hi - lo >= 1:
        mid = lo + (hi - lo) // 2
        yield from _oddeven_merge_sort(lo, mid)
        yield from _oddeven_merge_sort(mid + 1, hi)
        yield from _oddeven_merge(lo, hi, 1)


_SORT16 = tuple(_oddeven_merge_sort(0, PEER_TOPK - 1))


def _cmpx(v, i, j):
    a, b = v[i], v[j]
    if b is None:
        return
    if a is None:
        v[i], v[j] = b, None
        return
    v[i], v[j] = jnp.maximum(a, b), jnp.minimum(a, b)


def _top16_replicated(v):
    v = list(v)
    for i, j in _SORT16:
        _cmpx(v, i, j)
    for shift in (4, 2, 1):
        part = [None if a is None else pltpu.roll(a, shift, 0) for a in v]
        merged = []
        for i in range(PEER_TOPK):
            a, b = v[i], part[PEER_TOPK - 1 - i]
            merged.append(b if a is None else a if b is None else jnp.maximum(a, b))
        for stride in (8, 4, 2, 1):
            for i in range(PEER_TOPK):
                if not i & stride:
                    _cmpx(merged, i, i + stride)
        v = merged
    return v


def _topk_kernel(st_ref, e1_ref, e2_ref, tau_ref, *, tl):
    sub = lax.broadcasted_iota(jnp.int32, (SUBLANES, tl), 0)

    def spread(vals):
        out = vals[SUBLANES - 1]
        for s in range(SUBLANES - 2, -1, -1):
            out = jnp.where(sub == s, vals[s], out)
        return out

    def body(h, tau_acc):
        s1 = st_ref[2 * h]
        s2 = st_ref[2 * h + 1]
        a = _top16_replicated([s1[SUBLANES * i:SUBLANES * (i + 1)] for i in range(N_KEYS // SUBLANES)])
        b = _top16_replicated([s2[SUBLANES * i:SUBLANES * (i + 1)] for i in range(N_KEYS // SUBLANES)])
        b_lo, b_hi = spread(b[:SUBLANES]), spread(b[SUBLANES:])
        cand = [a[0] + b_lo, a[0] + b_hi] + [a[i] + b_lo for i in range(1, SUBLANES)] + [spread(a[SUBLANES:]) + b[0]]
        best = _top16_replicated(cand + [None] * (PEER_TOPK - len(cand)))
        z = jnp.exp(best[0] - best[0])
        for kk in range(1, PEER_TOPK):
            z = z + jnp.exp(best[kk] - best[0])
        reps = N_KEYS // SUBLANES
        e1_ref[h] = jnp.exp(s1 - jnp.concatenate([a[0]] * reps, axis=0))
        e2_ref[h] = jnp.exp(s2 - jnp.concatenate([b[0]] * reps, axis=0)) / jnp.concatenate([z] * reps, axis=0)
        return jnp.where(sub == h, best[PEER_TOPK - 1], tau_acc)

    tau_ref[...] = lax.fori_loop(0, PEER_HEADS, body, jnp.zeros((SUBLANES, tl), F32))


def _topk(st, tl):
    nhc, nk, t = st.shape
    blk = lambda n: pl.BlockSpec((n, nk, tl), lambda i: (0, 0, i))
    return pl.pallas_call(
        functools.partial(_topk_kernel, tl=tl),
        grid=(t // tl,),
        in_specs=[blk(nhc)],
        out_specs=[blk(PEER_HEADS), blk(PEER_HEADS), pl.BlockSpec((PEER_HEADS, tl), lambda i: (0, i))],
        out_shape=[jax.ShapeDtypeStruct((PEER_HEADS, nk, t), F32)] * 2 + [jax.ShapeDtypeStruct((PEER_HEADS, t), F32)],
        compiler_params=_cparams("arbitrary"),
        name="topk",
    )(st)


def _peer_kernel(xt_ref, u_ref, vt_ref, st_ref, e1_ref, e2_ref, tau_ref, h_ref, y_ref, acc_ref, *, eb, tt):
    ej = pl.program_id(1)

    @pl.when(ej == 0)
    def _():
        acc_ref[...] = jnp.zeros_like(acc_ref)

    ht = _dot(u_ref[...], xt_ref[...])
    acts = []
    for j in range(eb // N_KEYS):
        a = ej * (eb // N_KEYS) + j
        wt = jnp.zeros((N_KEYS, tt), F32)
        for h in range(PEER_HEADS):
            s = st_ref[2 * h, pl.ds(a, 1), :] + st_ref[2 * h + 1]
            val = e1_ref[h, pl.ds(a, 1), :] * e2_ref[h]
            wt = wt + jnp.where(s >= tau_ref[h:h + 1, :], val, 0.0)
        hj = ht[N_KEYS * j:N_KEYS * (j + 1)]
        gelu = 0.5 * hj * (1.0 + lax.erf(hj * (2.0 ** -0.5)))
        acts.append((wt * gelu).astype(BF16))
    acc_ref[...] += _dot(vt_ref[...], jnp.concatenate(acts, axis=0))

    @pl.when(ej == pl.num_programs(1) - 1)
    def _():
        y_ref[...] = h_ref[...] + acc_ref[...].T


def _peer(xt, u, vt, st, e1, e2, tau, h, tt, eb):
    d, t = xt.shape
    ne = u.shape[0]
    tok3 = lambda n: pl.BlockSpec((n, N_KEYS, tt), lambda i, j: (0, 0, i))
    return pl.pallas_call(
        functools.partial(_peer_kernel, eb=eb, tt=tt),
        grid=(t // tt, ne // eb),
        in_specs=[pl.BlockSpec((d, tt), lambda i, j: (0, i)),
                  pl.BlockSpec((eb, d), lambda i, j: (j, 0)),
                  pl.BlockSpec((d, eb), lambda i, j: (0, j)),
                  tok3(2 * PEER_HEADS), tok3(PEER_HEADS), tok3(PEER_HEADS),
                  pl.BlockSpec((PEER_HEADS, tt), lambda i, j: (0, i)),
                  pl.BlockSpec((tt, d), lambda i, j: (i, 0))],
        out_specs=pl.BlockSpec((tt, d), lambda i, j: (i, 0)),
        out_shape=jax.ShapeDtypeStruct((t, d), F32),
        scratch_shapes=[pltpu.VMEM((d, tt), F32)],
        compiler_params=_cparams("arbitrary", "arbitrary"),
        name="peer",
    )(xt, u, vt, st, e1, e2, tau, h)


def _gate_kernel(lf_ref, later_ref, same_ref, after_ref, tot_ref, *, tp):
    cat = jnp.concatenate(_split3(lf_ref[...]), axis=0)
    a = _dot(cat, later_ref[...])
    t = _dot(cat, same_ref[...])
    after_ref[...] = a[:tp] + a[tp:2 * tp] + a[2 * tp:]
    tot_ref[...] = t[:tp] + t[tp:2 * tp] + t[2 * tp:]


def _gate_suffix(lf_flat, tp):
    n_pool, width = lf_flat.shape
    c_src = np.arange(width)[:, None]
    c_dst = np.arange(width)[None, :]
    same_head = (c_src % FOX_HEADS) == (c_dst % FOX_HEADS)
    later = jnp.asarray(same_head & (c_src > c_dst), BF16)
    same = jnp.asarray(same_head[:, :LANES], BF16)
    return pl.pallas_call(
        functools.partial(_gate_kernel, tp=tp),
        grid=(n_pool // tp,),
        in_specs=[pl.BlockSpec((tp, width), lambda i: (i, 0)), _const_spec(later.shape), _const_spec(same.shape)],
        out_specs=[pl.BlockSpec((tp, width), lambda i: (i, 0)), pl.BlockSpec((tp, LANES), lambda i: (i, 0))],
        out_shape=[jax.ShapeDtypeStruct((n_pool, width), F32), jax.ShapeDtypeStruct((n_pool, LANES), F32)],
        compiler_params=_cparams("arbitrary"),
        name="gate_suffix",
    )(lf_flat, later, same)


def _paged_kernel(pt_ref, q_ref, knew_ref, vnew_ref, lgc_ref, lgr_ref, *rest, fox, n_pages, pages_per_step, dec_seq,
                  past_len):
    npp = pages_per_step
    k_refs = rest[:npp]
    v_refs = rest[npp:2 * npp]
    rest = rest[2 * npp:]
    if fox:
        after_refs = rest[:npp]
        tot_refs = rest[npp:2 * npp]
        rest = rest[2 * npp:]
    o_ref, m_ref, l_ref, acc_ref, carry_ref = rest
    b = pl.program_id(0)
    j = pl.program_id(1)
    rows = dec_seq * SCORE_HEADS
    cols = PAGE_SIZE * SCORE_HEADS
    q = q_ref[0]
    r_iota = lax.broadcasted_iota(jnp.int32, (rows, 1), 0)
    r_head = r_iota % SCORE_HEADS
    r_t = r_iota // SCORE_HEADS
    rh4 = r_head % DIFF_HEADS
    slope = jnp.where(rh4 == 0, 2.0 ** -2, jnp.where(rh4 == 1, 2.0 ** -4, jnp.where(rh4 == 2, 2.0 ** -6, 2.0 ** -8)))

    def update(scores, values):
        m_prev = m_ref[...]
        m_new = m_prev
        for s in scores:
            m_new = jnp.maximum(m_new, jnp.max(s, axis=1, keepdims=True))
        alpha = jnp.exp(m_prev - m_new)
        l_new = alpha * l_ref[...]
        acc = alpha * acc_ref[...]
        for s, v in zip(scores, values):
            pexp = jnp.exp(s - m_new)
            l_new = l_new + jnp.sum(pexp, axis=1, keepdims=True)
            acc = acc + _dot(pexp.astype(BF16), v)
        m_ref[...] = m_new
        l_ref[...] = l_new
        acc_ref[...] = acc

    def values_of(v_ref):
        v = v_ref[0, 0]
        v = v.reshape(v.shape[0] * v.shape[1], v.shape[2])
        if fox:
            return v.astype(BF16)
        nv = v.shape[0]
        up = pltpu.roll(v, DIFF_HEADS, 0)
        down = pltpu.roll(v, nv - DIFF_HEADS, 0)
        sub = lax.broadcasted_iota(jnp.int32, v.shape, 0) % SUBLANES
        first = jnp.where(sub < DIFF_HEADS, v, up).reshape(nv // SUBLANES, SUBLANES, v.shape[1])
        second = jnp.where(sub < DIFF_HEADS, down, v).reshape(nv // SUBLANES, SUBLANES, v.shape[1])
        return jnp.stack([first, second], axis=1).reshape(2 * nv, v.shape[1]).astype(BF16)

    @pl.when(j == 0)
    def _():
        m_ref[...] = jnp.full_like(m_ref, NEG_INF)
        l_ref[...] = jnp.zeros_like(l_ref)
        acc_ref[...] = jnp.zeros_like(acc_ref)
        carry_ref[...] = jnp.zeros_like(carry_ref)
        s = _dot_nt(q, knew_ref[0])
        c_iota = lax.broadcasted_iota(jnp.int32, (1, rows), 1)
        c_head = c_iota % SCORE_HEADS
        c_tok = c_iota // SCORE_HEADS
        if fox:
            n_row = jnp.zeros((rows, 1), F32)
            n_col = jnp.zeros((1, rows), F32)
            for u in range(dec_seq):
                n_row = n_row + jnp.where(r_t >= u, lgc_ref[0][:, u:u + 1], 0.0)
                n_col = n_col + jnp.where(c_tok >= u, lgr_ref[0][u:u + 1, :], 0.0)
            s = s + (n_row - n_col)
        else:
            s = s - slope * (r_t - c_tok).astype(F32)
        s = jnp.where((c_head == r_head) & (c_tok <= r_t), s, NEG_INF)
        update([s], [vnew_ref[0]])

    @pl.when(j > 0)
    def _():
        c_iota = lax.broadcasted_iota(jnp.int32, (1, cols), 1)
        valid = (c_iota % SCORE_HEADS) == r_head
        c_tok = c_iota // SCORE_HEADS
        if fox:
            n_row = jnp.zeros((rows, 1), F32)
            for u in range(dec_seq):
                n_row = n_row + jnp.where(r_t >= u, lgc_ref[0][:, u:u + 1], 0.0)
        scores, values = [], []
        for i in range(npp):
            slot = n_pages - 1 - ((j - 1) * npp + i)
            k = k_refs[i][0, 0]
            s = _dot_nt(q, k.reshape(cols, HEAD_DIM).astype(BF16))
            if fox:
                prow = pt_ref[b * n_pages + slot] % SUBLANES
                after = after_refs[i][pl.ds(prow, 1), :] + carry_ref[...]
                tot = tot_refs[i][pl.ds(prow, 1), :]
                carry_ref[...] += jnp.concatenate([tot] * (cols // LANES), axis=1)
                s = s + (after + n_row)
            else:
                s = s - slope * (past_len - slot * PAGE_SIZE + r_t - c_tok).astype(F32)
            scores.append(jnp.where(valid, s, NEG_INF))
            values.append(values_of(v_refs[i]))
        update(scores, values)

    @pl.when(j == pl.num_programs(1) - 1)
    def _():
        o_ref[0] = acc_ref[...] / l_ref[...]


def _paged(page_table, q, knew, vnew, lgc, lgr, k_cache, v_cache, gate_after, gate_tot, layer, *, fox,
           pages_per_step, dec_seq):
    b, n_pages = page_table.shape
    npp = pages_per_step
    rows = q.shape[1]
    dv = vnew.shape[2]
    pt = page_table.reshape(-1)

    def slot_of(j, i):
        return n_pages - 1 - (jnp.maximum(j, 1) - 1) * npp - i

    def page_map(i):
        return lambda bi, j, pt_ref: (layer, pt_ref[bi * n_pages + slot_of(j, i)], 0, 0, 0)

    def gate_map(i):
        return lambda bi, j, pt_ref: (pt_ref[bi * n_pages + slot_of(j, i)] // SUBLANES, 0)

    per_seq = lambda a: pl.BlockSpec((1,) + a.shape[1:], lambda bi, j, pt_ref: (bi, 0, 0))
    in_specs = [per_seq(a) for a in (q, knew, vnew, lgc, lgr)]
    in_specs += [pl.BlockSpec((1, 1) + k_cache.shape[2:], page_map(i)) for i in range(npp)]
    in_specs += [pl.BlockSpec((1, 1) + v_cache.shape[2:], page_map(i)) for i in range(npp)]
    args = [q, knew, vnew, lgc, lgr] + [k_cache] * npp + [v_cache] * npp
    if fox:
        in_specs += [pl.BlockSpec((SUBLANES, gate_after.shape[1]), gate_map(i)) for i in range(npp)]
        in_specs += [pl.BlockSpec((SUBLANES, LANES), gate_map(i)) for i in range(npp)]
        args += [gate_after] * npp + [gate_tot] * npp
    grid_spec = pltpu.PrefetchScalarGridSpec(
        num_scalar_prefetch=1,
        grid=(b, n_pages // npp + 1),
        in_specs=in_specs,
        out_specs=pl.BlockSpec((1, rows, dv), lambda bi, j, pt_ref: (bi, 0, 0)),
        scratch_shapes=[pltpu.VMEM((rows, 1), F32), pltpu.VMEM((rows, 1), F32), pltpu.VMEM((rows, dv), F32),
                        pltpu.VMEM((1, PAGE_SIZE * SCORE_HEADS), F32)],
    )
    return pl.pallas_call(
        functools.partial(_paged_kernel, fox=fox, n_pages=n_pages, pages_per_step=npp, dec_seq=dec_seq,
                          past_len=n_pages * PAGE_SIZE),
        grid_spec=grid_spec,
        out_shape=jax.ShapeDtypeStruct((b, rows, dv), F32),
        compiler_params=_cparams("arbitrary", "arbitrary"),
        name="paged_fox" if fox else "paged_diff",
    )(pt, *args)


def _tile(n, target):
    t = min(n, target)
    while n % t:
        t -= 1
    return t


def _layer_weights(layer, d, norm_mix, norm_mem, w_in, b_f, fox_q_norm, fox_k_norm, diff_q_norm, diff_k_norm,
                   diff_lambda, diff_out_norm, w_mem_kv, mem_q_norm, mem_k_norm, w_proj_fox, w_proj_diff,
                   w_proj_mem, w_out, norm_ffn, w_peer_q, peer_subkeys, peer_u, peer_v):
    w = BRANCH_WIDTH
    l = layer
    win = w_in[l]
    ff0 = 3 * w
    pad_lanes = lambda a: jnp.pad(a, ((0, 0), (0, LANES - a.shape[1])))
    pqf, pkf, pqd, pkd = _placement_matrices()
    bd64 = jnp.asarray(np.kron(np.eye(SCORE_HEADS), np.full((HEAD_DIM, HEAD_DIM), 1.0 / HEAD_DIM)), BF16)
    tile8 = lambda g: jnp.tile(g[l], SCORE_HEADS)[None, :]
    return dict(
        nmix=norm_mix[l][None, :],
        wmain=jnp.concatenate([win[:, :ff0], win[:, ff0 + FOX_HEADS:]], axis=1).astype(BF16),
        wff=pad_lanes(win[:, ff0:ff0 + FOX_HEADS]).astype(BF16),
        bf=pad_lanes(b_f[l][None, :]),
        bd64=bd64,
        gfq=tile8(fox_q_norm), gfk=tile8(fox_k_norm), gdq=tile8(diff_q_norm), gdk=tile8(diff_k_norm),
        gmq=mem_q_norm[l][None, :],
        pqf=pqf, pkf=pkf, pqd=pqd, pkd=pkd,
        nmem=norm_mem[l][None, :], wmemkv=w_mem_kv[l].astype(BF16), gmk=mem_k_norm[l][None, :],
        dlam=diff_lambda[l], gdo=diff_out_norm[l][None, :],
        wpf=w_proj_fox[l].astype(BF16), wpd=w_proj_diff[l].astype(BF16), wpm=w_proj_mem[l].astype(BF16),
        wout=w_out[l].astype(BF16), nffn=norm_ffn[l][None, :], wq=w_peer_q[l].astype(BF16),
        subk=peer_subkeys[l].astype(BF16),
        u=peer_u[l].astype(BF16), vt=peer_v[l].T.astype(BF16),
        lam_init=0.8 - 0.6 * float(np.exp(-0.3 * l)),
    )


def _ffn_tail(x, ofox, od, omem, gates, wts):
    t, d = x.shape
    h, xnt, st = _merge(x, ofox, od, omem, gates, wts, _tile(t, 256))
    e1, e2, tau = _topk(st, _tile(t, 256))
    return _peer(xnt, wts["u"], wts["vt"], st, e1, e2, tau, h, _tile(t, 512), _tile(wts["u"].shape[0], 1024))


def kernel(x_prompt, x_sample, mem_prompt, cache_fox_k, cache_fox_v, cache_fox_logf, cache_diff_k, cache_diff_v, cache_mem_k, cache_mem_v, page_table, norm_mix, norm_mem, w_in, b_f, fox_q_norm, fox_k_norm, diff_q_norm, diff_k_norm, diff_lambda, diff_out_norm, w_mem_kv, mem_q_norm, mem_k_norm, w_proj_fox, w_proj_diff, w_proj_mem, w_out, norm_ffn, w_peer_q, peer_subkeys, peer_u, peer_v):
    depth = w_in.shape[0]
    b, s, d = x_prompt.shape
    db, ds, _ = x_sample.shape
    n_mem = mem_prompt.shape[1]
    n_pool = cache_fox_k.shape[1]
    w = BRANCH_WIDTH
    xp, xs = x_prompt, x_sample
    outs = [[] for _ in range(12)]
    for l in range(depth):
        wts = _layer_weights(l, d, norm_mix, norm_mem, w_in, b_f, fox_q_norm, fox_k_norm, diff_q_norm, diff_k_norm,
                             diff_lambda, diff_out_norm, w_mem_kv, mem_q_norm, mem_k_norm, w_proj_fox, w_proj_diff,
                             w_proj_mem, w_out, norm_ffn, w_peer_q, peer_subkeys, peer_u, peer_v)

        pr = _proj(xp, wts, _tile(s, 256))
        mk, mv = _memkv(mem_prompt.reshape(b * n_mem, d), wts["nmem"], wts["wmemkv"], wts["gmk"], _tile(b * n_mem, 256))
        tq, tk = _tile(s, FLASH_TQ), _tile(s, FLASH_TK)
        ofox = _flash(pr["qf"], pr["kf"], pr["vfb"], group=2, dv=HEAD_DIM, tq=tq, tk=tk, name="flash_fox")
        od = _flash(pr["qd"], pr["kd"], pr["vdb"], group=1, dv=2 * HEAD_DIM, tq=tq, tk=tk, name="flash_diff")
        omem = _memattn(pr["mq"], mk.reshape(b, n_mem, w), mv.reshape(b, n_mem, w), _tile(s, 512))
        xp = _ffn_tail(xp.reshape(b * s, d), ofox, od, omem.reshape(b * s, w),
                       pr["gates"].reshape(b * s, N_BRANCH * d), wts).reshape(b, s, d)
        outs[0].append(pr["fk"].reshape(b, s, FOX_HEADS, HEAD_DIM))
        outs[1].append(pr["fv"].reshape(b, s, FOX_HEADS, HEAD_DIM))
        outs[2].append(pr["logf"])
        outs[3].append(pr["dk"].reshape(b, s, 2 * DIFF_HEADS, HEAD_DIM))
        outs[4].append(pr["dv"].reshape(b, s, DIFF_HEADS, 2 * HEAD_DIM))
        outs[5].append(mk.reshape(b, n_mem, MEM_HEADS, MEM_HEAD_DIM))
        outs[6].append(mv.reshape(b, n_mem, MEM_HEADS, MEM_HEAD_DIM))

        ts = db * ds
        sr = _proj(xs.reshape(1, ts, d), wts, _tile(ts, 256))
        pad_new = SUBLANES - ds

        nr = ds * SCORE_HEADS

        def query_rows(qp):
            return qp[0, :, :, :HEAD_DIM].reshape(SCORE_HEADS, db, ds, HEAD_DIM).transpose(1, 2, 0, 3).reshape(db, nr, HEAD_DIM)

        def head_rows(a):
            return a.reshape(db, nr, HEAD_DIM).astype(BF16)

        lf_new = sr["logf"].reshape(db, ds, FOX_HEADS)
        lgc = jnp.broadcast_to(lf_new.transpose(0, 2, 1)[:, None], (db, ds, FOX_HEADS, ds)).reshape(db, nr, ds)
        lgr = jnp.broadcast_to(lf_new[:, :, None, :], (db, ds, ds, FOX_HEADS)).reshape(db, ds, nr)
        dv_new = jnp.broadcast_to(sr["dv"].reshape(db, ds, 1, DIFF_HEADS, 2 * HEAD_DIM),
                                  (db, ds, 2, DIFF_HEADS, 2 * HEAD_DIM)).reshape(db, nr, 2 * HEAD_DIM).astype(BF16)
        gate_after, gate_tot = _gate_suffix(cache_fox_logf[l].reshape(n_pool, PAGE_SIZE * FOX_HEADS), _tile(n_pool, 512))
        npp = _tile(page_table.shape[1], 8)
        ofs = _paged(page_table, query_rows(sr["qf"]), head_rows(sr["fk"]), head_rows(sr["fv"]), lgc, lgr,
                     cache_fox_k, cache_fox_v, gate_after, gate_tot, l, fox=True, pages_per_step=npp, dec_seq=ds)
        ods = _paged(page_table, query_rows(sr["qd"]), head_rows(sr["dk"]), dv_new, lgc, lgr,
                     cache_diff_k, cache_diff_v, None, None, l, fox=False, pages_per_step=npp, dec_seq=ds)
        ofs = ofs.reshape(ts, w)
        ods = ods.reshape(db, ds, 2, DIFF_HEADS * 2 * HEAD_DIM).reshape(ts, 2 * w)
        mqs = jnp.pad(sr["mq"].reshape(db, ds, w), ((0, 0), (0, pad_new), (0, 0)))
        oms = _memattn(mqs, cache_mem_k[l].reshape(db, n_mem, w), cache_mem_v[l].reshape(db, n_mem, w), SUBLANES)
        oms = oms[:, :ds].reshape(ts, w)
        xs = _ffn_tail(xs.reshape(ts, d), ofs, ods, oms, sr["gates"].reshape(ts, N_BRANCH * d), wts).reshape(db, ds, d)
        outs[7].append(sr["fk"].reshape(db, ds, FOX_HEADS, HEAD_DIM))
        outs[8].append(sr["fv"].reshape(db, ds, FOX_HEADS, HEAD_DIM))
        outs[9].append(sr["logf"].reshape(db, ds, FOX_HEADS))
        outs[10].append(sr["dk"].reshape(db, ds, 2 * DIFF_HEADS, HEAD_DIM))
        outs[11].append(sr["dv"].reshape(db, ds, DIFF_HEADS, 2 * HEAD_DIM))

    return (xp, xs) + tuple(jnp.stack(o) for o in outs)
```

```python
import functools

import numpy as np
import jax
import jax.numpy as jnp
from jax import lax
from jax.experimental import pallas as pl
from jax.experimental.pallas import tpu as pltpu

F32 = jnp.float32
BF16 = jnp.bfloat16

HEAD_DIM = 64
FOX_HEADS = 8
DIFF_HEADS = 4
MEM_HEADS = 4
MEM_HEAD_DIM = 128
N_BRANCH = 3
PAGE_SIZE = 128
BRANCH_WIDTH = 512
SCORE_HEADS = 8
PEER_HEADS = 8
PEER_TOPK = 16
N_KEYS = 128
RMS_EPS = 1e-6
ATTN_SCALE = HEAD_DIM ** -0.5
MEM_SCALE = MEM_HEAD_DIM ** -0.5
LANES = 128
SUBLANES = 8
VMEM_LIMIT_BYTES = 56 * 1024 * 1024
FLASH_TQ = 512
FLASH_TK = 1024
PAGES_PER_STEP = 16
NEG_INF = float("-inf")


def _cparams(*sem):
    return pltpu.CompilerParams(dimension_semantics=sem, vmem_limit_bytes=VMEM_LIMIT_BYTES)


def _const_spec(shape):
    nd = len(shape)
    return pl.BlockSpec(shape, lambda *_: (0,) * nd, pipeline_mode=pl.Buffered(1))


def _rms_scale(x):
    return x * lax.rsqrt(jnp.mean(x * x, axis=-1, keepdims=True) + RMS_EPS)


def _split3(x):
    hi = x.astype(BF16)
    r = x - hi.astype(F32)
    mid = r.astype(BF16)
    lo = (r - mid.astype(F32)).astype(BF16)
    return hi, mid, lo


def _dot(a, b):
    return jnp.dot(a, b, preferred_element_type=F32)


def _dot_nt(a, b):
    return lax.dot_general(a, b, (((1,), (1,)), ((), ())), preferred_element_type=F32)


def _proj_kernel(x_ref, nmix_ref, wmain_ref, wff_ref, bf_ref, bd64_ref,
                 gfq_ref, gfk_ref, gdq_ref, gdk_ref, gmq_ref,
                 pqf_ref, pkf_ref, pqd_ref, pkd_ref,
                 fk_ref, fv_ref, lf_ref, dk_ref, dv_ref, mq_ref, gates_ref,
                 qf_ref, kf_ref, qd_ref, kd_ref, vfb_ref, vdb_ref,
                 carry_ref, *, tt, d_model):
    i = pl.program_id(1)
    w = BRANCH_WIDTH
    x = x_ref[0]
    xn = (_rms_scale(x) * nmix_ref[...]).astype(BF16)

    def proj(c0, n):
        return _dot(xn, wmain_ref[:, c0:c0 + n])

    bd64 = bd64_ref[...]

    def headnorm64(z, g_ref):
        ms = _dot((z * z).astype(BF16), bd64)
        return z * lax.rsqrt(ms + RMS_EPS) * g_ref[...]

    fq = headnorm64(proj(0, w), gfq_ref)
    fk = headnorm64(proj(w, w), gfk_ref)
    fv = proj(2 * w, w)
    dq = headnorm64(proj(3 * w, w), gdq_ref)
    dk = headnorm64(proj(4 * w, w), gdk_ref)
    dv = proj(5 * w, w)
    zq = proj(6 * w, w)
    mq = jnp.concatenate(
        [_rms_scale(zq[:, MEM_HEAD_DIM * h:MEM_HEAD_DIM * (h + 1)]) * gmq_ref[...] for h in range(MEM_HEADS)], axis=1)
    gates = jax.nn.sigmoid(proj(7 * w, N_BRANCH * d_model))

    ff = _dot(xn, wff_ref[...]) + bf_ref[...]
    lf = jnp.minimum(ff, 0.0) - jnp.log1p(jnp.exp(-jnp.abs(ff)))

    @pl.when(i == 0)
    def _():
        carry_ref[...] = jnp.zeros_like(carry_ref)

    row = lax.broadcasted_iota(jnp.int32, (tt, tt), 0)
    col = lax.broadcasted_iota(jnp.int32, (tt, tt), 1)
    tri = jnp.where(col <= row, 1.0, 0.0).astype(BF16)
    c = _dot(tri, jnp.concatenate(_split3(lf), axis=1))
    fc = c[:, :LANES] + c[:, LANES:2 * LANES] + c[:, 2 * LANES:] + carry_ref[...]
    carry_ref[...] = fc[tt - 1:tt, :]

    lane = lax.broadcasted_iota(jnp.int32, (tt, LANES), 1)
    one_lane0 = jnp.where(lane == 0, 1.0, 0.0).astype(BF16)
    fcat = jnp.concatenate(_split3(fc) + (one_lane0,), axis=1)
    qb_fox = _dot(fcat, pqf_ref[...])
    kb_fox = _dot(fcat, pkf_ref[...])

    pos = i * tt + lax.broadcasted_iota(jnp.int32, (tt, LANES), 0)
    pos_hi = ((pos >> 7) << 7).astype(F32)
    pos_lo = (pos & 127).astype(F32)
    pcat = jnp.where(lane == 0, pos_hi, jnp.where(lane == 1, pos_lo, jnp.where(lane == 2, 1.0, 0.0))).astype(BF16)
    qb_diff = _dot(pcat, pqd_ref[...])
    kb_diff = _dot(pcat, pkd_ref[...])

    def pack(z, bias, out_ref, scale):
        for h in range(SCORE_HEADS):
            base = z[:, LANES * (h // 2):LANES * (h // 2 + 1)]
            if h % 2:
                base = pltpu.roll(base, HEAD_DIM, 1)
            if scale != 1.0:
                base = base * scale
            out_ref[0, h] = jnp.where(lane < HEAD_DIM, base, bias[:, LANES * h:LANES * (h + 1)]).astype(BF16)

    pack(fq, qb_fox, qf_ref, ATTN_SCALE)
    pack(fk, kb_fox, kf_ref, 1.0)
    pack(dq, qb_diff, qd_ref, ATTN_SCALE)
    pack(dk, kb_diff, kd_ref, 1.0)

    fk_ref[0] = fk
    fv_ref[0] = fv
    lf_ref[0] = lf[:, :FOX_HEADS]
    dk_ref[0] = dk
    dv_ref[0] = dv
    mq_ref[0] = mq
    gates_ref[0] = gates
    vfb_ref[0] = fv.T.astype(BF16)
    vdb_ref[0] = dv.T.astype(BF16)


def _placement_matrices():
    n = SCORE_HEADS * LANES
    pqf = np.zeros((4 * LANES, n), np.float32)
    pkf = np.zeros((4 * LANES, n), np.float32)
    pqd = np.zeros((LANES, n), np.float32)
    pkd = np.zeros((LANES, n), np.float32)
    for h in range(SCORE_HEADS):
        c = h * LANES + HEAD_DIM
        for part in range(3):
            pqf[part * LANES + h, c + part] = 1.0
            pqf[3 * LANES, c + 3 + part] = 1.0
            pkf[3 * LANES, c + part] = 1.0
            pkf[part * LANES + h, c + 3 + part] = -1.0
        slope = 2.0 ** (-8.0 * ((h % DIFF_HEADS) + 1) / DIFF_HEADS)
        pqd[0, c] = -slope
        pqd[1, c + 1] = -slope
        pqd[2, c + 2] = 1.0
        pqd[2, c + 3] = 1.0
        pkd[2, c] = 1.0
        pkd[2, c + 1] = 1.0
        pkd[0, c + 2] = slope
        pkd[1, c + 3] = slope
    return [jnp.asarray(m, BF16) for m in (pqf, pkf, pqd, pkd)]


def _proj(x, weights, tt):
    b, s, d = x.shape
    w = BRANCH_WIDTH
    nt = s // tt
    tok = lambda width: pl.BlockSpec((1, tt, width), lambda bi, i: (bi, i, 0))
    head = pl.BlockSpec((1, SCORE_HEADS, tt, LANES), lambda bi, i: (bi, 0, i, 0))
    consts = [weights[k] for k in ("nmix", "wmain", "wff", "bf", "bd64", "gfq", "gfk", "gdq", "gdk", "gmq",
                                   "pqf", "pkf", "pqd", "pkd")]
    names = ("fk", "fv", "logf", "dk", "dv", "mq", "gates", "qf", "kf", "qd", "kd", "vfb", "vdb")
    shapes = ([jax.ShapeDtypeStruct((b, s, w), F32)] * 2 + [jax.ShapeDtypeStruct((b, s, FOX_HEADS), F32)]
              + [jax.ShapeDtypeStruct((b, s, w), F32)] * 3 + [jax.ShapeDtypeStruct((b, s, N_BRANCH * d), F32)]
              + [jax.ShapeDtypeStruct((b, SCORE_HEADS, s, LANES), BF16)] * 4 + [jax.ShapeDtypeStruct((b, w, s), BF16)] * 2)
    specs = ([tok(w)] * 2 + [tok(FOX_HEADS)] + [tok(w)] * 3 + [tok(N_BRANCH * d)] + [head] * 4
             + [pl.BlockSpec((1, w, tt), lambda bi, i: (bi, 0, i))] * 2)
    outs = pl.pallas_call(
        functools.partial(_proj_kernel, tt=tt, d_model=d),
        grid=(b, nt),
        in_specs=[tok(d)] + [_const_spec(c.shape) for c in consts],
        out_specs=specs,
        out_shape=shapes,
        scratch_shapes=[pltpu.VMEM((1, LANES), F32)],
        compiler_params=_cparams("arbitrary", "arbitrary"),
        name="proj",
    )(x, *consts)
    return dict(zip(names, outs))


def _flash_kernel(qi_ref, kj_ref, fl_ref, q_ref, k_ref, vt_ref, o_ref, m_ref, l_ref, acc_ref, *, group, tq, tk, dv):
    p = pl.program_id(2)
    qi = qi_ref[p]
    kj = kj_ref[p]
    flags = fl_ref[p]

    @pl.when(kj == 0)
    def _():
        m_ref[...] = jnp.full_like(m_ref, NEG_INF)
        l_ref[...] = jnp.zeros_like(l_ref)
        acc_ref[...] = jnp.zeros_like(acc_ref)

    def step(masked):
        for g in range(group):
            st = _dot_nt(k_ref[0, g], q_ref[0, g])
            if masked:
                key = kj * tk + lax.broadcasted_iota(jnp.int32, st.shape, 0)
                qry = qi * tq + lax.broadcasted_iota(jnp.int32, st.shape, 1)
                st = jnp.where(key <= qry, st, NEG_INF)
            m_prev = m_ref[g]
            m_new = jnp.maximum(m_prev, jnp.max(st, axis=0, keepdims=True))
            alpha = jnp.exp(m_prev - m_new)
            pt = jnp.exp(st - m_new)
            l_ref[g] = alpha * l_ref[g] + jnp.sum(pt, axis=0, keepdims=True)
            acc_ref[g] = alpha * acc_ref[g] + _dot(vt_ref[0, g * dv:(g + 1) * dv, :], pt.astype(BF16))
            m_ref[g] = m_new

    @pl.when((flags & 1) == 0)
    def _():
        step(False)

    @pl.when((flags & 1) == 1)
    def _():
        step(True)

    @pl.when((flags & 2) == 2)
    def _():
        for g in range(group):
            o_ref[0, g * dv:(g + 1) * dv, :] = acc_ref[g] / l_ref[g]


def _flash(q, k, vt, *, group, dv, tq, tk, name):
    b, hs, s, _ = q.shape
    ng = hs // group
    v_blocks = vt.shape[1] // (group * dv)
    pairs = [(i, j) for i in range(s // tq) for j in range((i * tq + tq - 1) // tk + 1)]
    qi_tab = jnp.asarray([i for i, _ in pairs], jnp.int32)
    kj_tab = jnp.asarray([j for _, j in pairs], jnp.int32)
    fl_tab = jnp.asarray([((j + 1) * tk - 1 > i * tq) + 2 * (j == (i * tq + tq - 1) // tk) for i, j in pairs], jnp.int32)
    grid_spec = pltpu.PrefetchScalarGridSpec(
        num_scalar_prefetch=3,
        grid=(b, ng, len(pairs)),
        in_specs=[
            pl.BlockSpec((1, group, tq, LANES), lambda bi, g, p, qt, kt, ft: (bi, g, qt[p], 0)),
            pl.BlockSpec((1, group, tk, LANES), lambda bi, g, p, qt, kt, ft: (bi, g, kt[p], 0)),
            pl.BlockSpec((1, group * dv, tk), lambda bi, g, p, qt, kt, ft: (bi, g % v_blocks, kt[p])),
        ],
        out_specs=pl.BlockSpec((1, group * dv, tq), lambda bi, g, p, qt, kt, ft: (bi, g, qt[p])),
        scratch_shapes=[pltpu.VMEM((group, 1, tq), F32), pltpu.VMEM((group, 1, tq), F32),
                        pltpu.VMEM((group, dv, tq), F32)],
    )
    return pl.pallas_call(
        functools.partial(_flash_kernel, group=group, tq=tq, tk=tk, dv=dv),
        grid_spec=grid_spec,
        out_shape=jax.ShapeDtypeStruct((b, hs * dv, s), F32),
        compiler_params=_cparams("arbitrary", "arbitrary", "arbitrary"),
        name=name,
    )(qi_tab, kj_tab, fl_tab, q, k, vt)


def _memkv_kernel(x_ref, nmem_ref, w_ref, gk_ref, mk_ref, mv_ref):
    xn = (_rms_scale(x_ref[...]) * nmem_ref[...]).astype(BF16)
    z = _dot(xn, w_ref[...])
    mk_ref[...] = jnp.concatenate(
        [_rms_scale(z[:, MEM_HEAD_DIM * h:MEM_HEAD_DIM * (h + 1)]) * gk_ref[...] for h in range(MEM_HEADS)], axis=1)
    mv_ref[...] = z[:, BRANCH_WIDTH:]


def _memkv(mem, nmem, w_kv, gk, tm):
    t, d = mem.shape
    w = BRANCH_WIDTH
    return pl.pallas_call(
        _memkv_kernel,
        grid=(t // tm,),
        in_specs=[pl.BlockSpec((tm, d), lambda i: (i, 0)), _const_spec(nmem.shape), _const_spec(w_kv.shape),
                  _const_spec(gk.shape)],
        out_specs=[pl.BlockSpec((tm, w), lambda i: (i, 0))] * 2,
        out_shape=[jax.ShapeDtypeStruct((t, w), F32)] * 2,
        compiler_params=_cparams("arbitrary"),
        name="memkv",
    )(mem, nmem, w_kv, gk)


def _memattn_kernel(q_ref, k_ref, v_ref, o_ref):
    outs = []
    for h in range(MEM_HEADS):
        sl = slice(MEM_HEAD_DIM * h, MEM_HEAD_DIM * (h + 1))
        q = q_ref[0][:, sl].astype(BF16)
        if len(k_ref.shape) == 4:
            k = k_ref[0, :, h, :].astype(BF16)
            v = v_ref[0, :, h, :].astype(BF16)
        else:
            k = k_ref[0][:, sl].astype(BF16)
            v = v_ref[0][:, sl].astype(BF16)
        s = _dot_nt(q, k) * MEM_SCALE
        e = jnp.exp(s - jnp.max(s, axis=1, keepdims=True))
        outs.append(_dot(e.astype(BF16), v) / jnp.sum(e, axis=1, keepdims=True))
    o_ref[0] = jnp.concatenate(outs, axis=1)


def _memattn(q, k, v, tq):
    bm, tm, w = q.shape
    kv_spec = pl.BlockSpec((1,) + k.shape[1:], lambda b, i: (b,) + (0,) * (k.ndim - 1))
    return pl.pallas_call(
        _memattn_kernel,
        grid=(bm, tm // tq),
        in_specs=[pl.BlockSpec((1, tq, w), lambda b, i: (b, i, 0)), kv_spec, kv_spec],
        out_specs=pl.BlockSpec((1, tq, w), lambda b, i: (b, i, 0)),
        out_shape=jax.ShapeDtypeStruct((bm, tm, w), F32),
        compiler_params=_cparams("arbitrary", "arbitrary"),
        name="memattn",
    )(q, k, v)


def _merge_kernel(x_ref, ofox_ref, od_ref, omem_ref, gates_ref, dlam_ref, gdo_ref,
                  wpf_ref, wpd_ref, wpm_ref, wout_ref, nffn_ref, wq_ref, subk_ref,
                  h_ref, xnt_ref, st_ref, *, d_model, lam_init, attn_t):
    w = BRANCH_WIDTH
    dl = dlam_ref[...]
    lam = (jnp.exp(jnp.sum(dl[0:1] * dl[1:2], axis=1, keepdims=True))
           - jnp.exp(jnp.sum(dl[2:3] * dl[3:4], axis=1, keepdims=True)) + lam_init)
    if attn_t:
        od = od_ref[0].T
        ofox = ofox_ref[0].T
    else:
        od = od_ref[...]
        ofox = ofox_ref[...]
    o = od[:, :w] - lam * od[:, w:]
    hw = 2 * HEAD_DIM
    odn = jnp.concatenate(
        [_rms_scale(o[:, hw * h:hw * (h + 1)]) * gdo_ref[...] * (1.0 - lam_init) for h in range(DIFF_HEADS)], axis=1)
    b_fox = _dot(ofox.astype(BF16), wpf_ref[...])
    b_diff = _dot(odn.astype(BF16), wpd_ref[...])
    b_mem = _dot(omem_ref[...].astype(BF16), wpm_ref[...])
    gates = gates_ref[...]
    m = (gates[:, :d_model] * b_fox + gates[:, d_model:2 * d_model] * b_diff + gates[:, 2 * d_model:] * b_mem)
    h = x_ref[...] + _dot(m.astype(BF16), wout_ref[...])
    h_ref[...] = h
    xn = _rms_scale(h) * nffn_ref[...]
    xnb = xn.astype(BF16)
    xnt_ref[...] = xn.T.astype(BF16)
    q = _dot(xnb, wq_ref[...])
    for hc in range(2 * PEER_HEADS):
        qhc = q[:, N_KEYS * hc:N_KEYS * (hc + 1)].astype(BF16)
        st_ref[hc] = _dot_nt(subk_ref[hc % 2], qhc)


def _merge(x, ofox, od, omem, gates, weights, tt):
    t, d = x.shape
    w = BRANCH_WIDTH
    consts = [weights[k] for k in ("dlam", "gdo", "wpf", "wpd", "wpm", "wout", "nffn", "wq", "subk")]
    rows = lambda width: pl.BlockSpec((tt, width), lambda i: (i, 0))
    attn_t = ofox.ndim == 3
    if attn_t:
        nts = ofox.shape[2] // tt
        attn = lambda width: pl.BlockSpec((1, width, tt), lambda i: (i // nts, 0, i % nts))
    else:
        attn = rows
    return pl.pallas_call(
        functools.partial(_merge_kernel, d_model=d, lam_init=weights["lam_init"], attn_t=attn_t),
        grid=(t // tt,),
        in_specs=[rows(d), attn(w), attn(2 * w), rows(w), rows(N_BRANCH * d)] + [_const_spec(c.shape) for c in consts],
        out_specs=[rows(d), pl.BlockSpec((d, tt), lambda i: (0, i)),
                   pl.BlockSpec((2 * PEER_HEADS, N_KEYS, tt), lambda i: (0, 0, i))],
        out_shape=[jax.ShapeDtypeStruct((t, d), F32), jax.ShapeDtypeStruct((d, t), BF16),
                   jax.ShapeDtypeStruct((2 * PEER_HEADS, N_KEYS, t), F32)],
        compiler_params=_cparams("arbitrary"),
        name="merge",
    )(x, ofox, od, omem, gates, *consts)


def _oddeven_merge(lo, hi, r):
    step = r * 2
    if step < hi - lo:
        yield from _oddeven_merge(lo, hi, step)
        yield from _oddeven_merge(lo + r, hi, step)
        yield from [(i, i + r) for i in range(lo + r, hi - r, step)]
    else:
        yield (lo, lo + r)


def _oddeven_merge_sort(lo, hi):
    if hi - lo >= 1:
        mid = lo + (hi - lo) // 2
        yield from _oddeven_merge_sort(lo, mid)
        yield from _oddeven_merge_sort(mid + 1, hi)
        yield from _oddeven_merge(lo, hi, 1)


_SORT16 = tuple(_oddeven_merge_sort(0, PEER_TOPK - 1))


def _cmpx(v, i, j):
    a, b = v[i], v[j]
    if b is None:
        return
    if a is None:
        v[i], v[j] = b, None
        return
    v[i], v[j] = jnp.maximum(a, b), jnp.minimum(a, b)


def _top16_replicated(v):
    v = list(v)
    for i, j in _SORT16:
        _cmpx(v, i, j)
    for shift in (4, 2, 1):
        part = [None if a is None else pltpu.roll(a, shift, 0) for a in v]
        merged = []
        for i in range(PEER_TOPK):
            a, b = v[i], part[PEER_TOPK - 1 - i]
            merged.append(b if a is None else a if b is None else jnp.maximum(a, b))
        for stride in (8, 4, 2, 1):
            for i in range(PEER_TOPK):
                if not i & stride:
                    _cmpx(merged, i, i + stride)
        v = merged
    return v


def _topk_kernel(st_ref, e1_ref, e2_ref, tau_ref, *, tl):
    sub = lax.broadcasted_iota(jnp.int32, (SUBLANES, tl), 0)

    def spread(vals):
        out = vals[SUBLANES - 1]
        for s in range(SUBLANES - 2, -1, -1):
            out = jnp.where(sub == s, vals[s], out)
        return out

    def body(h, tau_acc):
        s1 = st_ref[2 * h]
        s2 = st_ref[2 * h + 1]
        a = _top16_replicated([s1[SUBLANES * i:SUBLANES * (i + 1)] for i in range(N_KEYS // SUBLANES)])
        b = _top16_replicated([s2[SUBLANES * i:SUBLANES * (i + 1)] for i in range(N_KEYS // SUBLANES)])
        b_lo, b_hi = spread(b[:SUBLANES]), spread(b[SUBLANES:])
        cand = [a[0] + b_lo, a[0] + b_hi] + [a[i] + b_lo for i in range(1, SUBLANES)] + [spread(a[SUBLANES:]) + b[0]]
        best = _top16_replicated(cand + [None] * (PEER_TOPK - len(cand)))
        z = jnp.exp(best[0] - best[0])
        for kk in range(1, PEER_TOPK):
            z = z + jnp.exp(best[kk] - best[0])
        reps = N_KEYS // SUBLANES
        e1_ref[h] = jnp.exp(s1 - jnp.concatenate([a[0]] * reps, axis=0))
        e2_ref[h] = jnp.exp(s2 - jnp.concatenate([b[0]] * reps, axis=0)) / jnp.concatenate([z] * reps, axis=0)
        return jnp.where(sub == h, best[PEER_TOPK - 1], tau_acc)

    tau_ref[...] = lax.fori_loop(0, PEER_HEADS, body, jnp.zeros((SUBLANES, tl), F32))


def _topk(st, tl):
    nhc, nk, t = st.shape
    blk = lambda n: pl.BlockSpec((n, nk, tl), lambda i: (0, 0, i))
    return pl.pallas_call(
        functools.partial(_topk_kernel, tl=tl),
        grid=(t // tl,),
        in_specs=[blk(nhc)],
        out_specs=[blk(PEER_HEADS), blk(PEER_HEADS), pl.BlockSpec((PEER_HEADS, tl), lambda i: (0, i))],
        out_shape=[jax.ShapeDtypeStruct((PEER_HEADS, nk, t), F32)] * 2 + [jax.ShapeDtypeStruct((PEER_HEADS, t), F32)],
        compiler_params=_cparams("arbitrary"),
        name="topk",
    )(st)


def _peer_kernel(xt_ref, u_ref, vt_ref, st_ref, e1_ref, e2_ref, tau_ref, h_ref, y_ref, acc_ref, *, eb, tt):
    ej = pl.program_id(1)

    @pl.when(ej == 0)
    def _():
        acc_ref[...] = jnp.zeros_like(acc_ref)

    ht = _dot(u_ref[...], xt_ref[...])
    acts = []
    for j in range(eb // N_KEYS):
        a = ej * (eb // N_KEYS) + j
        wt = jnp.zeros((N_KEYS, tt), F32)
        for h in range(PEER_HEADS):
            s = st_ref[2 * h, pl.ds(a, 1), :] + st_ref[2 * h + 1]
            val = e1_ref[h, pl.ds(a, 1), :] * e2_ref[h]
            wt = wt + jnp.where(s >= tau_ref[h:h + 1, :], val, 0.0)
        hj = ht[N_KEYS * j:N_KEYS * (j + 1)]
        gelu = 0.5 * hj * (1.0 + lax.erf(hj * (2.0 ** -0.5)))
        acts.append((wt * gelu).astype(BF16))
    acc_ref[...] += _dot(vt_ref[...], jnp.concatenate(acts, axis=0))

    @pl.when(ej == pl.num_programs(1) - 1)
    def _():
        y_ref[...] = h_ref[...] + acc_ref[...].T


def _peer(xt, u, vt, st, e1, e2, tau, h, tt, eb):
    d, t = xt.shape
    ne = u.shape[0]
    tok3 = lambda n: pl.BlockSpec((n, N_KEYS, tt), lambda i, j: (0, 0, i))
    return pl.pallas_call(
        functools.partial(_peer_kernel, eb=eb, tt=tt),
        grid=(t // tt, ne // eb),
        in_specs=[pl.BlockSpec((d, tt), lambda i, j: (0, i)),
                  pl.BlockSpec((eb, d), lambda i, j: (j, 0)),
                  pl.BlockSpec((d, eb), lambda i, j: (0, j)),
                  tok3(2 * PEER_HEADS), tok3(PEER_HEADS), tok3(PEER_HEADS),
                  pl.BlockSpec((PEER_HEADS, tt), lambda i, j: (0, i)),
                  pl.BlockSpec((tt, d), lambda i, j: (i, 0))],
        out_specs=pl.BlockSpec((tt, d), lambda i, j: (i, 0)),
        out_shape=jax.ShapeDtypeStruct((t, d), F32),
        scratch_shapes=[pltpu.VMEM((d, tt), F32)],
        compiler_params=_cparams("arbitrary", "arbitrary"),
        name="peer",
    )(xt, u, vt, st, e1, e2, tau, h)


def _paged_kernel(pt_ref, q_ref, knew_ref, vnew_ref, lg_ref, *rest, fox, n_pages, pages_per_step, dec_seq, past_len):
    npp = pages_per_step
    if not fox:
        spread_ref, rest = rest[0], rest[1:]
    k_refs = rest[:npp]
    v_refs = rest[npp:2 * npp]
    rest = rest[2 * npp:]
    lf_refs = rest[:npp] if fox else ()
    o_ref, m_ref, l_ref, acc_ref, carry_ref = rest[-5:]
    j = pl.program_id(1)
    rows = dec_seq * SCORE_HEADS
    q = q_ref[0]
    r_iota = lax.broadcasted_iota(jnp.int32, (rows, 1), 0)
    if fox:
        r_t = r_iota // SCORE_HEADS
        lg = jnp.concatenate([lg_ref[0]] * dec_seq, axis=0)
        n_row = jnp.zeros((rows, 1), F32)
        for u in range(dec_seq):
            n_row = n_row + jnp.where(r_t >= u, lg[:, u:u + 1], 0.0)
    else:
        r_t = r_iota % dec_seq
        r_h = r_iota // (2 * dec_seq)
        slope = jnp.where(r_h == 0, 2.0 ** -2, jnp.where(r_h == 1, 2.0 ** -4, jnp.where(r_h == 2, 2.0 ** -6, 2.0 ** -8)))

    def head_rows(p, h):
        return p[SUBLANES * h:SUBLANES * (h + 1)]

    def update(scores, weigh):
        m_prev = m_ref[...]
        m_new = m_prev
        for s in scores:
            m_new = jnp.maximum(m_new, jnp.max(s, axis=1, keepdims=True))
        alpha = jnp.exp(m_prev - m_new)
        l_new = alpha * l_ref[...]
        acc = alpha * acc_ref[...]
        for s, pv in zip(scores, weigh):
            pexp = jnp.exp(s - m_new)
            l_new = l_new + jnp.sum(pexp, axis=1, keepdims=True)
            acc = acc + pv(pexp.astype(BF16))
        m_ref[...] = m_new
        l_ref[...] = l_new
        acc_ref[...] = acc

    @pl.when(j == 0)
    def _():
        m_ref[...] = jnp.full_like(m_ref, NEG_INF)
        l_ref[...] = jnp.zeros_like(l_ref)
        acc_ref[...] = jnp.zeros_like(acc_ref)
        carry_ref[...] = jnp.zeros_like(carry_ref)
        s = _dot_nt(q, knew_ref[0])
        col = lax.broadcasted_iota(jnp.int32, (1, s.shape[1]), 1)
        if fox:
            n_col = jnp.zeros(s.shape, F32)
            for u in range(dec_seq):
                n_col = n_col + jnp.where(col >= u, lg[:, u:u + 1], 0.0)
            s = s + (n_row - n_col)
            pv = lambda p: _dot(p, vnew_ref[0])
        else:
            s = s - slope * (r_t - col).astype(F32)
            pv = lambda p: jnp.concatenate([_dot(head_rows(p, h), vnew_ref[0, h]) for h in range(DIFF_HEADS)], axis=0)
        update([jnp.where(col <= r_t, s, NEG_INF)], [pv])

    @pl.when(j > 0)
    def _():
        tok = lax.broadcasted_iota(jnp.int32, (1, PAGE_SIZE), 1)
        if fox:
            ri = lax.broadcasted_iota(jnp.int32, (PAGE_SIZE, PAGE_SIZE), 0)
            ci = lax.broadcasted_iota(jnp.int32, (PAGE_SIZE, PAGE_SIZE), 1)
            later = jnp.where(ri > ci, 1.0, 0.0).astype(BF16)
        else:
            vcol = lax.broadcasted_iota(jnp.int32, (1, PAGE_SIZE * DIFF_HEADS), 1)
            own_head = (vcol % DIFF_HEADS) == r_h
        scores, weigh = [], []
        for i in range(npp):
            slot = n_pages - 1 - ((j - 1) * npp + i)
            s = _dot(q, k_refs[i][0].astype(BF16))
            if fox:
                lf = lf_refs[i][0]
                c = _dot(jnp.concatenate(_split3(lf), axis=0), later)
                after = c[:FOX_HEADS] + c[FOX_HEADS:2 * FOX_HEADS] + c[2 * FOX_HEADS:] + carry_ref[...]
                carry_ref[...] = after[:, 0:1] + lf[:, 0:1]
                s = s + (jnp.concatenate([after] * dec_seq, axis=0) + n_row)
                weigh.append(lambda p, v_ref=v_refs[i]: _dot_nt(p, v_ref[0].astype(BF16)))
            else:
                s = s - slope * (past_len + r_t - slot * PAGE_SIZE - tok).astype(F32)
                def pv(p, v_ref=v_refs[i]):
                    v = v_ref[0].reshape(PAGE_SIZE * DIFF_HEADS, 2 * HEAD_DIM).astype(BF16)
                    return _dot(jnp.where(own_head, _dot(p, spread_ref[...]), 0.0).astype(BF16), v)
                weigh.append(pv)
            scores.append(s)
        update(scores, weigh)

    @pl.when(j == pl.num_programs(1) - 1)
    def _():
        o_ref[0] = acc_ref[...] / l_ref[...]


def _paged(page_table, q, knew, vnew, lg, kt_pool, v_pool, lf_pool, *, fox, pages_per_step, dec_seq):
    b, n_pages = page_table.shape
    npp = pages_per_step
    rows = q.shape[1]
    dv = BRANCH_WIDTH if fox else 2 * HEAD_DIM
    pt = page_table.reshape(-1)

    def page_map(i, nd):
        def index(bi, j, pt_ref):
            slot = n_pages - 1 - (jnp.maximum(j, 1) - 1) * npp - i
            return (pt_ref[bi * n_pages + slot],) + (0,) * nd
        return index

    def per_seq(a):
        nd = a.ndim - 1
        return pl.BlockSpec((1,) + a.shape[1:], lambda bi, j, pt_ref: (bi,) + (0,) * nd)

    def paged(a):
        return [pl.BlockSpec((1,) + a.shape[1:], page_map(i, a.ndim - 1)) for i in range(npp)]

    in_specs = [per_seq(a) for a in (q, knew, vnew, lg)]
    args = [q, knew, vnew, lg]
    if not fox:
        tok = np.arange(PAGE_SIZE)[:, None]
        col = np.arange(PAGE_SIZE * DIFF_HEADS)[None, :]
        spread = jnp.asarray(col // DIFF_HEADS == tok, BF16)
        in_specs.append(pl.BlockSpec(spread.shape, lambda bi, j, pt_ref: (0, 0), pipeline_mode=pl.Buffered(1)))
        args.append(spread)
    in_specs += paged(kt_pool) + paged(v_pool)
    args += [kt_pool] * npp + [v_pool] * npp
    if fox:
        in_specs += paged(lf_pool)
        args += [lf_pool] * npp
    grid_spec = pltpu.PrefetchScalarGridSpec(
        num_scalar_prefetch=1,
        grid=(b, n_pages // npp + 1),
        in_specs=in_specs,
        out_specs=pl.BlockSpec((1, rows, dv), lambda bi, j, pt_ref: (bi, 0, 0)),
        scratch_shapes=[pltpu.VMEM((rows, 1), F32), pltpu.VMEM((rows, 1), F32), pltpu.VMEM((rows, dv), F32),
                        pltpu.VMEM((SCORE_HEADS, 1), F32)],
    )
    return pl.pallas_call(
        functools.partial(_paged_kernel, fox=fox, n_pages=n_pages, pages_per_step=npp, dec_seq=dec_seq,
                          past_len=n_pages * PAGE_SIZE),
        grid_spec=grid_spec,
        out_shape=jax.ShapeDtypeStruct((b, rows, dv), F32),
        compiler_params=_cparams("arbitrary", "arbitrary"),
        name="paged_fox" if fox else "paged_diff",
    )(pt, *args)


def _tile(n, target):
    t = min(n, target)
    while n % t:
        t -= 1
    return t


def _layer_weights(layer, d, norm_mix, norm_mem, w_in, b_f, fox_q_norm, fox_k_norm, diff_q_norm, diff_k_norm,
                   diff_lambda, diff_out_norm, w_mem_kv, mem_q_norm, mem_k_norm, w_proj_fox, w_proj_diff,
                   w_proj_mem, w_out, norm_ffn, w_peer_q, peer_subkeys, peer_u, peer_v):
    w = BRANCH_WIDTH
    l = layer
    win = w_in[l]
    ff0 = 3 * w
    pad_lanes = lambda a: jnp.pad(a, ((0, 0), (0, LANES - a.shape[1])))
    pqf, pkf, pqd, pkd = _placement_matrices()
    bd64 = jnp.asarray(np.kron(np.eye(SCORE_HEADS), np.full((HEAD_DIM, HEAD_DIM), 1.0 / HEAD_DIM)), BF16)
    tile8 = lambda g: jnp.tile(g[l], SCORE_HEADS)[None, :]
    return dict(
        nmix=norm_mix[l][None, :],
        wmain=jnp.concatenate([win[:, :ff0], win[:, ff0 + FOX_HEADS:]], axis=1).astype(BF16),
        wff=pad_lanes(win[:, ff0:ff0 + FOX_HEADS]).astype(BF16),
        bf=pad_lanes(b_f[l][None, :]),
        bd64=bd64,
        gfq=tile8(fox_q_norm), gfk=tile8(fox_k_norm), gdq=tile8(diff_q_norm), gdk=tile8(diff_k_norm),
        gmq=mem_q_norm[l][None, :],
        pqf=pqf, pkf=pkf, pqd=pqd, pkd=pkd,
        nmem=norm_mem[l][None, :], wmemkv=w_mem_kv[l].astype(BF16), gmk=mem_k_norm[l][None, :],
        dlam=diff_lambda[l], gdo=diff_out_norm[l][None, :],
        wpf=w_proj_fox[l].astype(BF16), wpd=w_proj_diff[l].astype(BF16), wpm=w_proj_mem[l].astype(BF16),
        wout=w_out[l].astype(BF16), nffn=norm_ffn[l][None, :], wq=w_peer_q[l].astype(BF16),
        subk=peer_subkeys[l].astype(BF16),
        u=peer_u[l].astype(BF16), vt=peer_v[l].T.astype(BF16),
        lam_init=0.8 - 0.6 * float(np.exp(-0.3 * l)),
    )


def _ffn_tail(x, ofox, od, omem, gates, wts):
    t, d = x.shape
    h, xnt, st = _merge(x, ofox, od, omem, gates, wts, _tile(t, 256))
    e1, e2, tau = _topk(st, _tile(t, 256))
    return _peer(xnt, wts["u"], wts["vt"], st, e1, e2, tau, h, _tile(t, 512), _tile(wts["u"].shape[0], 1024))


def kernel(x_prompt, x_sample, mem_prompt, cache_fox_k, cache_fox_v, cache_fox_logf, cache_diff_k, cache_diff_v, cache_mem_k, cache_mem_v, page_table, norm_mix, norm_mem, w_in, b_f, fox_q_norm, fox_k_norm, diff_q_norm, diff_k_norm, diff_lambda, diff_out_norm, w_mem_kv, mem_q_norm, mem_k_norm, w_proj_fox, w_proj_diff, w_proj_mem, w_out, norm_ffn, w_peer_q, peer_subkeys, peer_u, peer_v):
    depth = w_in.shape[0]
    b, s, d = x_prompt.shape
    db, ds, _ = x_sample.shape
    n_mem = mem_prompt.shape[1]
    n_pool = cache_fox_k.shape[1]
    w = BRANCH_WIDTH
    xp, xs = x_prompt, x_sample
    outs = [[] for _ in range(12)]
    for l in range(depth):
        wts = _layer_weights(l, d, norm_mix, norm_mem, w_in, b_f, fox_q_norm, fox_k_norm, diff_q_norm, diff_k_norm,
                             diff_lambda, diff_out_norm, w_mem_kv, mem_q_norm, mem_k_norm, w_proj_fox, w_proj_diff,
                             w_proj_mem, w_out, norm_ffn, w_peer_q, peer_subkeys, peer_u, peer_v)

        pr = _proj(xp, wts, _tile(s, 256))
        mk, mv = _memkv(mem_prompt.reshape(b * n_mem, d), wts["nmem"], wts["wmemkv"], wts["gmk"], _tile(b * n_mem, 256))
        tq, tk = _tile(s, FLASH_TQ), _tile(s, FLASH_TK)
        ofox = _flash(pr["qf"], pr["kf"], pr["vfb"], group=2, dv=HEAD_DIM, tq=tq, tk=tk, name="flash_fox")
        od = _flash(pr["qd"], pr["kd"], pr["vdb"], group=1, dv=2 * HEAD_DIM, tq=tq, tk=tk, name="flash_diff")
        omem = _memattn(pr["mq"], mk.reshape(b, n_mem, w), mv.reshape(b, n_mem, w), _tile(s, 512))
        xp = _ffn_tail(xp.reshape(b * s, d), ofox, od, omem.reshape(b * s, w),
                       pr["gates"].reshape(b * s, N_BRANCH * d), wts).reshape(b, s, d)
        outs[0].append(pr["fk"].reshape(b, s, FOX_HEADS, HEAD_DIM))
        outs[1].append(pr["fv"].reshape(b, s, FOX_HEADS, HEAD_DIM))
        outs[2].append(pr["logf"])
        outs[3].append(pr["dk"].reshape(b, s, 2 * DIFF_HEADS, HEAD_DIM))
        outs[4].append(pr["dv"].reshape(b, s, DIFF_HEADS, 2 * HEAD_DIM))
        outs[5].append(mk.reshape(b, n_mem, MEM_HEADS, MEM_HEAD_DIM))
        outs[6].append(mv.reshape(b, n_mem, MEM_HEADS, MEM_HEAD_DIM))

        ts = db * ds
        sr = _proj(xs.reshape(1, ts, d), wts, _tile(ts, 256))
        pad_new = SUBLANES - ds
        pad_new_bf16 = 2 * SUBLANES - ds
        assert 2 * ds == SUBLANES, "the differential decode rows put one value head's (map, token) rows in one tile"
        nr = ds * SCORE_HEADS
        eye = jnp.eye(SCORE_HEADS, dtype=BF16)

        def block_diag_q(qp, order):
            qh = qp[0, :, :, :HEAD_DIM].reshape(SCORE_HEADS, db, ds, HEAD_DIM).transpose(1, 2, 0, 3)
            qbd = qh[:, :, :, None, :] * eye[None, None, :, :, None]
            return order(qbd).reshape(db, nr, w)

        def new_rows(a):
            return jnp.pad(a.reshape(db, ds, w), ((0, 0), (0, pad_new_bf16), (0, 0))).astype(BF16)

        def key_major(c):
            return c.transpose(0, 2, 3, 1).reshape(n_pool, w, PAGE_SIZE)

        lg = sr["logf"].reshape(db, ds, FOX_HEADS).transpose(0, 2, 1)
        npp = _tile(page_table.shape[1], PAGES_PER_STEP)
        ofs = _paged(page_table, block_diag_q(sr["qf"], lambda x: x), new_rows(sr["fk"]), new_rows(sr["fv"]), lg,
                     key_major(cache_fox_k[l]), key_major(cache_fox_v[l]), cache_fox_logf[l].transpose(0, 2, 1),
                     fox=True, pages_per_step=npp, dec_seq=ds)
        hidx = jnp.arange(FOX_HEADS)
        ofs = ofs.reshape(db, ds, FOX_HEADS, FOX_HEADS, HEAD_DIM)[:, :, hidx, hidx, :].reshape(ts, w)
        by_head = lambda x: x.reshape(db, ds, 2, DIFF_HEADS, SCORE_HEADS, HEAD_DIM).transpose(0, 3, 2, 1, 4, 5)
        dv_new = jnp.pad(sr["dv"].reshape(db, ds, DIFF_HEADS, 2 * HEAD_DIM).transpose(0, 2, 1, 3),
                         ((0, 0), (0, 0), (0, pad_new_bf16), (0, 0))).astype(BF16)
        ods = _paged(page_table, block_diag_q(sr["qd"], by_head), new_rows(sr["dk"]), dv_new, lg,
                     key_major(cache_diff_k[l]), cache_diff_v[l], None, fox=False, pages_per_step=npp, dec_seq=ds)
        ods = ods.reshape(db, DIFF_HEADS, 2, ds, 2 * HEAD_DIM).transpose(0, 3, 2, 1, 4).reshape(ts, 2 * w)
        mqs = jnp.pad(sr["mq"].reshape(db, ds, w), ((0, 0), (0, pad_new), (0, 0)))
        oms = _memattn(mqs, cache_mem_k[l], cache_mem_v[l], SUBLANES)
        oms = oms[:, :ds].reshape(ts, w)
        xs = _ffn_tail(xs.reshape(ts, d), ofs, ods, oms, sr["gates"].reshape(ts, N_BRANCH * d), wts).reshape(db, ds, d)
        outs[7].append(sr["fk"].reshape(db, ds, FOX_HEADS, HEAD_DIM))
        outs[8].append(sr["fv"].reshape(db, ds, FOX_HEADS, HEAD_DIM))
        outs[9].append(sr["logf"].reshape(db, ds, FOX_HEADS))
        outs[10].append(sr["dk"].reshape(db, ds, 2 * DIFF_HEADS, HEAD_DIM))
        outs[11].append(sr["dv"].reshape(db, ds, DIFF_HEADS, 2 * HEAD_DIM))

    return (xp, xs) + tuple(jnp.stack(o) for o in outs)
```

```python
import functools

import numpy as np
import jax
import jax.numpy as jnp
from jax import lax
from jax.experimental import pallas as pl
from jax.experimental.pallas import tpu as pltpu

F32 = jnp.float32
BF16 = jnp.bfloat16

HEAD_DIM = 64
FOX_HEADS = 8
DIFF_HEADS = 4
MEM_HEADS = 4
MEM_HEAD_DIM = 128
N_BRANCH = 3
PAGE_SIZE = 128
BRANCH_WIDTH = 512
SCORE_HEADS = 8
PEER_HEADS = 8
PEER_TOPK = 16
N_KEYS = 128
RMS_EPS = 1e-6
ATTN_SCALE = HEAD_DIM ** -0.5
MEM_SCALE = MEM_HEAD_DIM ** -0.5
LANES = 128
SUBLANES = 8
VMEM_LIMIT_BYTES = 56 * 1024 * 1024
FLASH_TQ = 512
FLASH_TK = 1024
PAGES_PER_STEP = 32
LOG2E = 1.4426950408889634
NEG_INF = float("-inf")


def _cparams(*sem):
    return pltpu.CompilerParams(dimension_semantics=sem, vmem_limit_bytes=VMEM_LIMIT_BYTES)


def _const_spec(shape):
    nd = len(shape)
    return pl.BlockSpec(shape, lambda *_: (0,) * nd, pipeline_mode=pl.Buffered(1))


def _rms_scale(x):
    return x * lax.rsqrt(jnp.mean(x * x, axis=-1, keepdims=True) + RMS_EPS)


def _split3(x):
    hi = x.astype(BF16)
    r = x - hi.astype(F32)
    mid = r.astype(BF16)
    lo = (r - mid.astype(F32)).astype(BF16)
    return hi, mid, lo


def _dot(a, b):
    return jnp.dot(a, b, preferred_element_type=F32)


def _dot_nt(a, b):
    return lax.dot_general(a, b, (((1,), (1,)), ((), ())), preferred_element_type=F32)


def _proj_kernel(x_ref, nmix_ref, wmain_ref, wff_ref, bf_ref, bd64_ref,
                 gfq_ref, gfk_ref, gdq_ref, gdk_ref, gmq_ref,
                 pqf_ref, pkf_ref, pqd_ref, pkd_ref,
                 fk_ref, fv_ref, lf_ref, dk_ref, dv_ref, mq_ref, gates_ref,
                 qf_ref, kf_ref, qd_ref, kd_ref, vfb_ref, vdb_ref,
                 carry_ref, *, tt, d_model, gate_scale):
    i = pl.program_id(1)
    w = BRANCH_WIDTH
    x = x_ref[0]
    xn = (_rms_scale(x) * nmix_ref[...]).astype(BF16)

    def proj(c0, n):
        return _dot(xn, wmain_ref[:, c0:c0 + n])

    bd64 = bd64_ref[...]

    def headnorm64(z, g_ref):
        ms = _dot((z * z).astype(BF16), bd64)
        return z * lax.rsqrt(ms + RMS_EPS) * g_ref[...]

    fq = headnorm64(proj(0, w), gfq_ref)
    fk = headnorm64(proj(w, w), gfk_ref)
    fv = proj(2 * w, w)
    dq = headnorm64(proj(3 * w, w), gdq_ref)
    dk = headnorm64(proj(4 * w, w), gdk_ref)
    dv = proj(5 * w, w)
    zq = proj(6 * w, w)
    mq = jnp.concatenate(
        [_rms_scale(zq[:, MEM_HEAD_DIM * h:MEM_HEAD_DIM * (h + 1)]) * gmq_ref[...] for h in range(MEM_HEADS)], axis=1)
    gates = jax.nn.sigmoid(proj(7 * w, N_BRANCH * d_model))

    ff = _dot(xn, wff_ref[...]) + bf_ref[...]
    lf = jnp.minimum(ff, 0.0) - jnp.log1p(jnp.exp(-jnp.abs(ff)))

    @pl.when(i == 0)
    def _():
        carry_ref[...] = jnp.zeros_like(carry_ref)

    row = lax.broadcasted_iota(jnp.int32, (tt, tt), 0)
    col = lax.broadcasted_iota(jnp.int32, (tt, tt), 1)
    tri = jnp.where(col <= row, 1.0, 0.0).astype(BF16)
    c = _dot(tri, jnp.concatenate(_split3(lf), axis=1))
    fc = c[:, :LANES] + c[:, LANES:2 * LANES] + c[:, 2 * LANES:] + carry_ref[...]
    carry_ref[...] = fc[tt - 1:tt, :]

    lane = lax.broadcasted_iota(jnp.int32, (tt, LANES), 1)
    one_lane0 = jnp.where(lane == 0, 1.0, 0.0).astype(BF16)
    fcat = jnp.concatenate(_split3(fc * gate_scale) + (one_lane0,), axis=1)
    qb_fox = _dot(fcat, pqf_ref[...])
    kb_fox = _dot(fcat, pkf_ref[...])

    pos = i * tt + lax.broadcasted_iota(jnp.int32, (tt, LANES), 0)
    pos_hi = ((pos >> 7) << 7).astype(F32)
    pos_lo = (pos & 127).astype(F32)
    pcat = jnp.where(lane == 0, pos_hi, jnp.where(lane == 1, pos_lo, jnp.where(lane == 2, 1.0, 0.0))).astype(BF16)
    qb_diff = _dot(pcat, pqd_ref[...])
    kb_diff = _dot(pcat, pkd_ref[...])

    def pack(z, bias, out_ref, scale):
        for h in range(SCORE_HEADS):
            base = z[:, LANES * (h // 2):LANES * (h // 2 + 1)]
            if h % 2:
                base = pltpu.roll(base, HEAD_DIM, 1)
            if scale != 1.0:
                base = base * scale
            out_ref[0, h] = jnp.where(lane < HEAD_DIM, base, bias[:, LANES * h:LANES * (h + 1)]).astype(BF16)

    pack(fq, qb_fox, qf_ref, ATTN_SCALE * gate_scale)
    pack(fk, kb_fox, kf_ref, 1.0)
    pack(dq, qb_diff, qd_ref, ATTN_SCALE * gate_scale)
    pack(dk, kb_diff, kd_ref, 1.0)

    fk_ref[0] = fk
    fv_ref[0] = fv
    lf_ref[0] = lf[:, :FOX_HEADS]
    dk_ref[0] = dk
    dv_ref[0] = dv
    mq_ref[0] = mq
    gates_ref[0] = gates
    vfb_ref[0] = fv.T.astype(BF16)
    vdb_ref[0] = dv.T.astype(BF16)


def _placement_matrices():
    n = SCORE_HEADS * LANES
    pqf = np.zeros((4 * LANES, n), np.float32)
    pkf = np.zeros((4 * LANES, n), np.float32)
    pqd = np.zeros((LANES, n), np.float32)
    pkd = np.zeros((LANES, n), np.float32)
    for h in range(SCORE_HEADS):
        c = h * LANES + HEAD_DIM
        for part in range(3):
            pqf[part * LANES + h, c + part] = 1.0
            pqf[3 * LANES, c + 3 + part] = 1.0
            pkf[3 * LANES, c + part] = 1.0
            pkf[part * LANES + h, c + 3 + part] = -1.0
        coef = np.float32(2.0 ** (-8.0 * ((h % DIFF_HEADS) + 1) / DIFF_HEADS)) * np.float32(LOG2E)
        for part in range(3):
            piece = np.float32(np.asarray(coef).astype(jnp.bfloat16))
            coef = np.float32(coef - piece)
            pqd[2, c + 2 * part] = pqd[2, c + 2 * part + 1] = piece
            pkd[0, c + 2 * part] = pkd[1, c + 2 * part + 1] = 1.0
            pqd[0, c + 6 + 2 * part] = pqd[1, c + 7 + 2 * part] = -1.0
            pkd[2, c + 6 + 2 * part] = pkd[2, c + 7 + 2 * part] = piece
    return [jnp.asarray(m, BF16) for m in (pqf, pkf, pqd, pkd)]


def _proj(x, weights, tt, gate_scale):
    b, s, d = x.shape
    w = BRANCH_WIDTH
    nt = s // tt
    tok = lambda width: pl.BlockSpec((1, tt, width), lambda bi, i: (bi, i, 0))
    head = pl.BlockSpec((1, SCORE_HEADS, tt, LANES), lambda bi, i: (bi, 0, i, 0))
    consts = [weights[k] for k in ("nmix", "wmain", "wff", "bf", "bd64", "gfq", "gfk", "gdq", "gdk", "gmq",
                                   "pqf", "pkf", "pqd", "pkd")]
    names = ("fk", "fv", "logf", "dk", "dv", "mq", "gates", "qf", "kf", "qd", "kd", "vfb", "vdb")
    shapes = ([jax.ShapeDtypeStruct((b, s, w), F32)] * 2 + [jax.ShapeDtypeStruct((b, s, FOX_HEADS), F32)]
              + [jax.ShapeDtypeStruct((b, s, w), F32)] * 3 + [jax.ShapeDtypeStruct((b, s, N_BRANCH * d), F32)]
              + [jax.ShapeDtypeStruct((b, SCORE_HEADS, s, LANES), BF16)] * 4 + [jax.ShapeDtypeStruct((b, w, s), BF16)] * 2)
    specs = ([tok(w)] * 2 + [tok(FOX_HEADS)] + [tok(w)] * 3 + [tok(N_BRANCH * d)] + [head] * 4
             + [pl.BlockSpec((1, w, tt), lambda bi, i: (bi, 0, i))] * 2)
    outs = pl.pallas_call(
        functools.partial(_proj_kernel, tt=tt, d_model=d, gate_scale=gate_scale),
        grid=(b, nt),
        in_specs=[tok(d)] + [_const_spec(c.shape) for c in consts],
        out_specs=specs,
        out_shape=shapes,
        scratch_shapes=[pltpu.VMEM((1, LANES), F32)],
        compiler_params=_cparams("arbitrary", "arbitrary"),
        name="proj",
    )(x, *consts)
    return dict(zip(names, outs))


def _flash_kernel(qi_ref, kj_ref, fl_ref, q_ref, k_ref, vt_ref, o_ref, m_ref, l_ref, acc_ref, *, group, tq, tk, dv):
    p = pl.program_id(2)
    qi = qi_ref[p]
    kj = kj_ref[p]
    flags = fl_ref[p]

    @pl.when(kj == 0)
    def _():
        m_ref[...] = jnp.full_like(m_ref, NEG_INF)
        l_ref[...] = jnp.zeros_like(l_ref)
        acc_ref[...] = jnp.zeros_like(acc_ref)

    def step(masked):
        for g in range(group):
            st = _dot_nt(k_ref[0, g], q_ref[0, g])
            if masked:
                key = kj * tk + lax.broadcasted_iota(jnp.int32, st.shape, 0)
                qry = qi * tq + lax.broadcasted_iota(jnp.int32, st.shape, 1)
                st = jnp.where(key <= qry, st, NEG_INF)
            m_prev = m_ref[g]
            m_new = jnp.maximum(m_prev, jnp.max(st, axis=0, keepdims=True))
            alpha = jnp.exp2(m_prev - m_new)
            pt = jnp.exp2(st - m_new)
            l_ref[g] = alpha * l_ref[g] + jnp.sum(pt, axis=0, keepdims=True)
            acc_ref[g] = alpha * acc_ref[g] + _dot(vt_ref[0, g * dv:(g + 1) * dv, :], pt.astype(BF16))
            m_ref[g] = m_new

    @pl.when((flags & 1) == 0)
    def _():
        step(False)

    @pl.when((flags & 1) == 1)
    def _():
        step(True)

    @pl.when((flags & 2) == 2)
    def _():
        for g in range(group):
            o_ref[0, g * dv:(g + 1) * dv, :] = acc_ref[g] / l_ref[g]


def _flash(q, k, vt, *, group, dv, tq, tk, name):
    b, hs, s, _ = q.shape
    ng = hs // group
    v_blocks = vt.shape[1] // (group * dv)
    pairs = [(i, j) for i in range(s // tq) for j in range((i * tq + tq - 1) // tk + 1)]
    qi_tab = jnp.asarray([i for i, _ in pairs], jnp.int32)
    kj_tab = jnp.asarray([j for _, j in pairs], jnp.int32)
    fl_tab = jnp.asarray([((j + 1) * tk - 1 > i * tq) + 2 * (j == (i * tq + tq - 1) // tk) for i, j in pairs], jnp.int32)
    grid_spec = pltpu.PrefetchScalarGridSpec(
        num_scalar_prefetch=3,
        grid=(b, ng, len(pairs)),
        in_specs=[
            pl.BlockSpec((1, group, tq, LANES), lambda bi, g, p, qt, kt, ft: (bi, g, qt[p], 0)),
            pl.BlockSpec((1, group, tk, LANES), lambda bi, g, p, qt, kt, ft: (bi, g, kt[p], 0)),
            pl.BlockSpec((1, group * dv, tk), lambda bi, g, p, qt, kt, ft: (bi, g % v_blocks, kt[p])),
        ],
        out_specs=pl.BlockSpec((1, group * dv, tq), lambda bi, g, p, qt, kt, ft: (bi, g, qt[p])),
        scratch_shapes=[pltpu.VMEM((group, 1, tq), F32), pltpu.VMEM((group, 1, tq), F32),
                        pltpu.VMEM((group, dv, tq), F32)],
    )
    return pl.pallas_call(
        functools.partial(_flash_kernel, group=group, tq=tq, tk=tk, dv=dv),
        grid_spec=grid_spec,
        out_shape=jax.ShapeDtypeStruct((b, hs * dv, s), F32),
        compiler_params=_cparams("arbitrary", "arbitrary", "arbitrary"),
        name=name,
    )(qi_tab, kj_tab, fl_tab, q, k, vt)


def _memkv_kernel(x_ref, nmem_ref, w_ref, gk_ref, mk_ref, mv_ref):
    xn = (_rms_scale(x_ref[...]) * nmem_ref[...]).astype(BF16)
    z = _dot(xn, w_ref[...])
    mk_ref[...] = jnp.concatenate(
        [_rms_scale(z[:, MEM_HEAD_DIM * h:MEM_HEAD_DIM * (h + 1)]) * gk_ref[...] for h in range(MEM_HEADS)], axis=1)
    mv_ref[...] = z[:, BRANCH_WIDTH:]


def _memkv(mem, nmem, w_kv, gk, tm):
    t, d = mem.shape
    w = BRANCH_WIDTH
    return pl.pallas_call(
        _memkv_kernel,
        grid=(t // tm,),
        in_specs=[pl.BlockSpec((tm, d), lambda i: (i, 0)), _const_spec(nmem.shape), _const_spec(w_kv.shape),
                  _const_spec(gk.shape)],
        out_specs=[pl.BlockSpec((tm, w), lambda i: (i, 0))] * 2,
        out_shape=[jax.ShapeDtypeStruct((t, w), F32)] * 2,
        compiler_params=_cparams("arbitrary"),
        name="memkv",
    )(mem, nmem, w_kv, gk)


def _memattn_kernel(q_ref, k_ref, v_ref, o_ref):
    outs = []
    for h in range(MEM_HEADS):
        sl = slice(MEM_HEAD_DIM * h, MEM_HEAD_DIM * (h + 1))
        q = q_ref[0][:, sl].astype(BF16)
        if len(k_ref.shape) == 4:
            k = k_ref[0, :, h, :].astype(BF16)
            v = v_ref[0, :, h, :].astype(BF16)
        else:
            k = k_ref[0][:, sl].astype(BF16)
            v = v_ref[0][:, sl].astype(BF16)
        s = _dot_nt(q, k) * MEM_SCALE
        e = jnp.exp(s - jnp.max(s, axis=1, keepdims=True))
        outs.append(_dot(e.astype(BF16), v) / jnp.sum(e, axis=1, keepdims=True))
    o_ref[0] = jnp.concatenate(outs, axis=1)


def _memattn(q, k, v, tq):
    bm, tm, w = q.shape
    kv_spec = pl.BlockSpec((1,) + k.shape[1:], lambda b, i: (b,) + (0,) * (k.ndim - 1))
    return pl.pallas_call(
        _memattn_kernel,
        grid=(bm, tm // tq),
        in_specs=[pl.BlockSpec((1, tq, w), lambda b, i: (b, i, 0)), kv_spec, kv_spec],
        out_specs=pl.BlockSpec((1, tq, w), lambda b, i: (b, i, 0)),
        out_shape=jax.ShapeDtypeStruct((bm, tm, w), F32),
        compiler_params=_cparams("arbitrary", "arbitrary"),
        name="memattn",
    )(q, k, v)


def _merge_kernel(x_ref, ofox_ref, od_ref, omem_ref, gates_ref, dlam_ref, gdo_ref,
                  wpf_ref, wpd_ref, wpm_ref, wout_ref, nffn_ref, wq_ref, subk_ref,
                  h_ref, xnt_ref, st_ref, *, d_model, lam_init, attn_t):
    w = BRANCH_WIDTH
    dl = dlam_ref[...]
    lam = (jnp.exp(jnp.sum(dl[0:1] * dl[1:2], axis=1, keepdims=True))
           - jnp.exp(jnp.sum(dl[2:3] * dl[3:4], axis=1, keepdims=True)) + lam_init)
    if attn_t:
        od = od_ref[0].T
        ofox = ofox_ref[0].T
    else:
        od = od_ref[...]
        ofox = ofox_ref[...]
    o = od[:, :w] - lam * od[:, w:]
    hw = 2 * HEAD_DIM
    odn = jnp.concatenate(
        [_rms_scale(o[:, hw * h:hw * (h + 1)]) * gdo_ref[...] * (1.0 - lam_init) for h in range(DIFF_HEADS)], axis=1)
    b_fox = _dot(ofox.astype(BF16), wpf_ref[...])
    b_diff = _dot(odn.astype(BF16), wpd_ref[...])
    b_mem = _dot(omem_ref[...].astype(BF16), wpm_ref[...])
    gates = gates_ref[...]
    m = (gates[:, :d_model] * b_fox + gates[:, d_model:2 * d_model] * b_diff + gates[:, 2 * d_model:] * b_mem)
    h = x_ref[...] + _dot(m.astype(BF16), wout_ref[...])
    h_ref[...] = h
    xn = _rms_scale(h) * nffn_ref[...]
    xnb = xn.astype(BF16)
    xnt_ref[...] = xn.T.astype(BF16)
    q = _dot(xnb, wq_ref[...])
    for hc in range(2 * PEER_HEADS):
        qhc = q[:, N_KEYS * hc:N_KEYS * (hc + 1)].astype(BF16)
        st_ref[hc] = _dot_nt(subk_ref[hc % 2], qhc)


def _merge(x, ofox, od, omem, gates, weights, tt):
    t, d = x.shape
    w = BRANCH_WIDTH
    consts = [weights[k] for k in ("dlam", "gdo", "wpf", "wpd", "wpm", "wout", "nffn", "wq", "subk")]
    rows = lambda width: pl.BlockSpec((tt, width), lambda i: (i, 0))
    attn_t = ofox.ndim == 3
    if attn_t:
        nts = ofox.shape[2] // tt
        attn = lambda width: pl.BlockSpec((1, width, tt), lambda i: (i // nts, 0, i % nts))
    else:
        attn = rows
    return pl.pallas_call(
        functools.partial(_merge_kernel, d_model=d, lam_init=weights["lam_init"], attn_t=attn_t),
        grid=(t // tt,),
        in_specs=[rows(d), attn(w), attn(2 * w), rows(w), rows(N_BRANCH * d)] + [_const_spec(c.shape) for c in consts],
        out_specs=[rows(d), pl.BlockSpec((d, tt), lambda i: (0, i)),
                   pl.BlockSpec((2 * PEER_HEADS, N_KEYS, tt), lambda i: (0, 0, i))],
        out_shape=[jax.ShapeDtypeStruct((t, d), F32), jax.ShapeDtypeStruct((d, t), BF16),
                   jax.ShapeDtypeStruct((2 * PEER_HEADS, N_KEYS, t), F32)],
        compiler_params=_cparams("arbitrary"),
        name="merge",
    )(x, ofox, od, omem, gates, *consts)


def _oddeven_merge(lo, hi, r):
    step = r * 2
    if step < hi - lo:
        yield from _oddeven_merge(lo, hi, step)
        yield from _oddeven_merge(lo + r, hi, step)
        yield from [(i, i + r) for i in range(lo + r, hi - r, step)]
    else:
        yield (lo, lo + r)


def _oddeven_merge_sort(lo, hi):
    if hi - lo >= 1:
        mid = lo + (hi - lo) // 2
        yield from _oddeven_merge_sort(lo, mid)
        yield from _oddeven_merge_sort(mid + 1, hi)
        yield from _oddeven_merge(lo, hi, 1)


_SORT16 = tuple(_oddeven_merge_sort(0, PEER_TOPK - 1))


def _cmpx(v, i, j):
    a, b = v[i], v[j]
    if b is None:
        return
    if a is None:
        v[i], v[j] = b, None
        return
    v[i], v[j] = jnp.maximum(a, b), jnp.minimum(a, b)


def _top16_replicated(v):
    v = list(v)
    for i, j in _SORT16:
        _cmpx(v, i, j)
    for shift in (4, 2, 1):
        part = [None if a is None else pltpu.roll(a, shift, 0) for a in v]
        merged = []
        for i in range(PEER_TOPK):
            a, b = v[i], part[PEER_TOPK - 1 - i]
            merged.append(b if a is None else a if b is None else jnp.maximum(a, b))
        for stride in (8, 4, 2, 1):
            for i in range(PEER_TOPK):
                if not i & stride:
                    _cmpx(merged, i, i + stride)
        v = merged
    return v


def _topk_kernel(st_ref, e1_ref, e2_ref, tau_ref, *, tl):
    sub = lax.broadcasted_iota(jnp.int32, (SUBLANES, tl), 0)

    def spread(vals):
        out = vals[SUBLANES - 1]
        for s in range(SUBLANES - 2, -1, -1):
            out = jnp.where(sub == s, vals[s], out)
        return out

    def body(h, tau_acc):
        s1 = st_ref[2 * h]
        s2 = st_ref[2 * h + 1]
        a = _top16_replicated([s1[SUBLANES * i:SUBLANES * (i + 1)] for i in range(N_KEYS // SUBLANES)])
        b = _top16_replicated([s2[SUBLANES * i:SUBLANES * (i + 1)] for i in range(N_KEYS // SUBLANES)])
        b_lo, b_hi = spread(b[:SUBLANES]), spread(b[SUBLANES:])
        cand = [a[0] + b_lo, a[0] + b_hi] + [a[i] + b_lo for i in range(1, SUBLANES)] + [spread(a[SUBLANES:]) + b[0]]
        best = _top16_replicated(cand + [None] * (PEER_TOPK - len(cand)))
        z = jnp.exp(best[0] - best[0])
        for kk in range(1, PEER_TOPK):
            z = z + jnp.exp(best[kk] - best[0])
        reps = N_KEYS // SUBLANES
        e1_ref[h] = jnp.exp(s1 - jnp.concatenate([a[0]] * reps, axis=0))
        e2_ref[h] = jnp.exp(s2 - jnp.concatenate([b[0]] * reps, axis=0)) / jnp.concatenate([z] * reps, axis=0)
        return jnp.where(sub == h, best[PEER_TOPK - 1], tau_acc)

    tau_ref[...] = lax.fori_loop(0, PEER_HEADS, body, jnp.zeros((SUBLANES, tl), F32))


def _topk(st, tl):
    nhc, nk, t = st.shape
    blk = lambda n: pl.BlockSpec((n, nk, tl), lambda i: (0, 0, i))
    return pl.pallas_call(
        functools.partial(_topk_kernel, tl=tl),
        grid=(t // tl,),
        in_specs=[blk(nhc)],
        out_specs=[blk(PEER_HEADS), blk(PEER_HEADS), pl.BlockSpec((PEER_HEADS, tl), lambda i: (0, i))],
        out_shape=[jax.ShapeDtypeStruct((PEER_HEADS, nk, t), F32)] * 2 + [jax.ShapeDtypeStruct((PEER_HEADS, t), F32)],
        compiler_params=_cparams("arbitrary"),
        name="topk",
    )(st)


def _peer_kernel(xt_ref, u_ref, vt_ref, st_ref, e1_ref, e2_ref, tau_ref, h_ref, y_ref, acc_ref, *, eb, tt):
    ej = pl.program_id(1)

    @pl.when(ej == 0)
    def _():
        acc_ref[...] = jnp.zeros_like(acc_ref)

    ht = _dot(u_ref[...], xt_ref[...])
    acts = []
    for j in range(eb // N_KEYS):
        a = ej * (eb // N_KEYS) + j
        wt = jnp.zeros((N_KEYS, tt), F32)
        for h in range(PEER_HEADS):
            s = st_ref[2 * h, pl.ds(a, 1), :] + st_ref[2 * h + 1]
            val = e1_ref[h, pl.ds(a, 1), :] * e2_ref[h]
            wt = wt + jnp.where(s >= tau_ref[h:h + 1, :], val, 0.0)
        hj = ht[N_KEYS * j:N_KEYS * (j + 1)]
        gelu = 0.5 * hj * (1.0 + lax.erf(hj * (2.0 ** -0.5)))
        acts.append((wt * gelu).astype(BF16))
    acc_ref[...] += _dot(vt_ref[...], jnp.concatenate(acts, axis=0))

    @pl.when(ej == pl.num_programs(1) - 1)
    def _():
        y_ref[...] = h_ref[...] + acc_ref[...].T


def _peer(xt, u, vt, st, e1, e2, tau, h, tt, eb):
    d, t = xt.shape
    ne = u.shape[0]
    tok3 = lambda n: pl.BlockSpec((n, N_KEYS, tt), lambda i, j: (0, 0, i))
    return pl.pallas_call(
        functools.partial(_peer_kernel, eb=eb, tt=tt),
        grid=(t // tt, ne // eb),
        in_specs=[pl.BlockSpec((d, tt), lambda i, j: (0, i)),
                  pl.BlockSpec((eb, d), lambda i, j: (j, 0)),
                  pl.BlockSpec((d, eb), lambda i, j: (0, j)),
                  tok3(2 * PEER_HEADS), tok3(PEER_HEADS), tok3(PEER_HEADS),
                  pl.BlockSpec((PEER_HEADS, tt), lambda i, j: (0, i)),
                  pl.BlockSpec((tt, d), lambda i, j: (i, 0))],
        out_specs=pl.BlockSpec((tt, d), lambda i, j: (i, 0)),
        out_shape=jax.ShapeDtypeStruct((t, d), F32),
        scratch_shapes=[pltpu.VMEM((d, tt), F32)],
        compiler_params=_cparams("arbitrary", "arbitrary"),
        name="peer",
    )(xt, u, vt, st, e1, e2, tau, h)


def _paged_kernel(pt_ref, q_ref, knew_ref, vnew_ref, lg_ref, *rest, fox, n_pages, pages_per_step, dec_seq, past_len):
    npp = pages_per_step
    if not fox:
        spread_ref, rest = rest[0], rest[1:]
    k_refs = rest[:npp]
    v_refs = rest[npp:2 * npp]
    rest = rest[2 * npp:]
    lf_refs = rest[:npp] if fox else ()
    o_ref, m_ref, l_ref, acc_ref, carry_ref = rest[-5:]
    j = pl.program_id(1)
    rows = dec_seq * SCORE_HEADS
    q = q_ref[0]
    r_iota = lax.broadcasted_iota(jnp.int32, (rows, 1), 0)
    if fox:
        r_t = r_iota // SCORE_HEADS
        lg = jnp.concatenate([lg_ref[0]] * dec_seq, axis=0)
        n_row = jnp.zeros((rows, 1), F32)
        for u in range(dec_seq):
            n_row = n_row + jnp.where(r_t >= u, lg[:, u:u + 1], 0.0)
    else:
        r_t = r_iota % dec_seq
        r_h = r_iota // (2 * dec_seq)
        slope = jnp.where(r_h == 0, 2.0 ** -2, jnp.where(r_h == 1, 2.0 ** -4, jnp.where(r_h == 2, 2.0 ** -6, 2.0 ** -8)))

    def head_rows(p, h):
        return p[SUBLANES * h:SUBLANES * (h + 1)]

    def update(scores, weigh):
        m_prev = m_ref[...]
        m_new = m_prev
        for s in scores:
            m_new = jnp.maximum(m_new, jnp.max(s, axis=1, keepdims=True))
        alpha = jnp.exp(m_prev - m_new)
        l_new = alpha * l_ref[...]
        acc = alpha * acc_ref[...]
        for s, pv in zip(scores, weigh):
            pexp = jnp.exp(s - m_new)
            l_new = l_new + jnp.sum(pexp, axis=1, keepdims=True)
            acc = acc + pv(pexp.astype(BF16))
        m_ref[...] = m_new
        l_ref[...] = l_new
        acc_ref[...] = acc

    @pl.when(j == 0)
    def _():
        m_ref[...] = jnp.full_like(m_ref, NEG_INF)
        l_ref[...] = jnp.zeros_like(l_ref)
        acc_ref[...] = jnp.zeros_like(acc_ref)
        carry_ref[...] = jnp.zeros_like(carry_ref)
        s = _dot_nt(q, knew_ref[0])
        col = lax.broadcasted_iota(jnp.int32, (1, s.shape[1]), 1)
        if fox:
            n_col = jnp.zeros(s.shape, F32)
            for u in range(dec_seq):
                n_col = n_col + jnp.where(col >= u, lg[:, u:u + 1], 0.0)
            s = s + (n_row - n_col)
            pv = lambda p: _dot(p, vnew_ref[0])
        else:
            s = s - slope * (r_t - col).astype(F32)
            pv = lambda p: jnp.concatenate([_dot(head_rows(p, h), vnew_ref[0, h]) for h in range(DIFF_HEADS)], axis=0)
        update([jnp.where(col <= r_t, s, NEG_INF)], [pv])

    @pl.when(j > 0)
    def _():
        tok = lax.broadcasted_iota(jnp.int32, (1, PAGE_SIZE), 1)
        if fox:
            ri = lax.broadcasted_iota(jnp.int32, (PAGE_SIZE, PAGE_SIZE), 0)
            ci = lax.broadcasted_iota(jnp.int32, (PAGE_SIZE, PAGE_SIZE), 1)
            later = jnp.where(ri > ci, 1.0, 0.0).astype(BF16)
        else:
            vcol = lax.broadcasted_iota(jnp.int32, (1, PAGE_SIZE * DIFF_HEADS), 1)
            own_head = (vcol % DIFF_HEADS) == r_h
        scores, weigh = [], []
        for i in range(npp):
            slot = n_pages - 1 - ((j - 1) * npp + i)
            s = _dot(q, k_refs[i][0].astype(BF16))
            if fox:
                lf = lf_refs[i][0]
                c = _dot(jnp.concatenate(_split3(lf), axis=0), later)
                after = c[:FOX_HEADS] + c[FOX_HEADS:2 * FOX_HEADS] + c[2 * FOX_HEADS:] + carry_ref[...]
                carry_ref[...] = after[:, 0:1] + lf[:, 0:1]
                s = s + (jnp.concatenate([after] * dec_seq, axis=0) + n_row)
                weigh.append(lambda p, v_ref=v_refs[i]: _dot_nt(p, v_ref[0].astype(BF16)))
            else:
                s = s - slope * (past_len + r_t - slot * PAGE_SIZE - tok).astype(F32)
                def pv(p, v_ref=v_refs[i]):
                    v = v_ref[0].reshape(PAGE_SIZE * DIFF_HEADS, 2 * HEAD_DIM).astype(BF16)
                    return _dot(jnp.where(own_head, _dot(p, spread_ref[...]), 0.0).astype(BF16), v)
                weigh.append(pv)
            scores.append(s)
        update(scores, weigh)

    @pl.when(j == pl.num_programs(1) - 1)
    def _():
        o_ref[0] = acc_ref[...] / l_ref[...]


def _paged(page_table, q, knew, vnew, lg, kt_pool, v_pool, lf_pool, *, fox, pages_per_step, dec_seq):
    b, n_pages = page_table.shape
    npp = pages_per_step
    rows = q.shape[1]
    dv = BRANCH_WIDTH if fox else 2 * HEAD_DIM
    pt = page_table.reshape(-1)

    def page_map(i, nd):
        def index(bi, j, pt_ref):
            slot = n_pages - 1 - (jnp.maximum(j, 1) - 1) * npp - i
            return (pt_ref[bi * n_pages + slot],) + (0,) * nd
        return index

    def per_seq(a):
        nd = a.ndim - 1
        return pl.BlockSpec((1,) + a.shape[1:], lambda bi, j, pt_ref: (bi,) + (0,) * nd)

    def paged(a):
        return [pl.BlockSpec((1,) + a.shape[1:], page_map(i, a.ndim - 1)) for i in range(npp)]

    in_specs = [per_seq(a) for a in (q, knew, vnew, lg)]
    args = [q, knew, vnew, lg]
    if not fox:
        tok = np.arange(PAGE_SIZE)[:, None]
        col = np.arange(PAGE_SIZE * DIFF_HEADS)[None, :]
        spread = jnp.asarray(col // DIFF_HEADS == tok, BF16)
        in_specs.append(pl.BlockSpec(spread.shape, lambda bi, j, pt_ref: (0, 0), pipeline_mode=pl.Buffered(1)))
        args.append(spread)
    in_specs += paged(kt_pool) + paged(v_pool)
    args += [kt_pool] * npp + [v_pool] * npp
    if fox:
        in_specs += paged(lf_pool)
        args += [lf_pool] * npp
    grid_spec = pltpu.PrefetchScalarGridSpec(
        num_scalar_prefetch=1,
        grid=(b, n_pages // npp + 1),
        in_specs=in_specs,
        out_specs=pl.BlockSpec((1, rows, dv), lambda bi, j, pt_ref: (bi, 0, 0)),
        scratch_shapes=[pltpu.VMEM((rows, 1), F32), pltpu.VMEM((rows, 1), F32), pltpu.VMEM((rows, dv), F32),
                        pltpu.VMEM((SCORE_HEADS, 1), F32)],
    )
    return pl.pallas_call(
        functools.partial(_paged_kernel, fox=fox, n_pages=n_pages, pages_per_step=npp, dec_seq=dec_seq,
                          past_len=n_pages * PAGE_SIZE),
        grid_spec=grid_spec,
        out_shape=jax.ShapeDtypeStruct((b, rows, dv), F32),
        compiler_params=_cparams("arbitrary", "arbitrary"),
        name="paged_fox" if fox else "paged_diff",
    )(pt, *args)


def _tile(n, target):
    t = min(n, target)
    while n % t:
        t -= 1
    return t


def _layer_weights(layer, d, norm_mix, norm_mem, w_in, b_f, fox_q_norm, fox_k_norm, diff_q_norm, diff_k_norm,
                   diff_lambda, diff_out_norm, w_mem_kv, mem_q_norm, mem_k_norm, w_proj_fox, w_proj_diff,
                   w_proj_mem, w_out, norm_ffn, w_peer_q, peer_subkeys, peer_u, peer_v):
    w = BRANCH_WIDTH
    l = layer
    win = w_in[l]
    ff0 = 3 * w
    pad_lanes = lambda a: jnp.pad(a, ((0, 0), (0, LANES - a.shape[1])))
    pqf, pkf, pqd, pkd = _placement_matrices()
    bd64 = jnp.asarray(np.kron(np.eye(SCORE_HEADS), np.full((HEAD_DIM, HEAD_DIM), 1.0 / HEAD_DIM)), BF16)
    tile8 = lambda g: jnp.tile(g[l], SCORE_HEADS)[None, :]
    return dict(
        nmix=norm_mix[l][None, :],
        wmain=jnp.concatenate([win[:, :ff0], win[:, ff0 + FOX_HEADS:]], axis=1).astype(BF16),
        wff=pad_lanes(win[:, ff0:ff0 + FOX_HEADS]).astype(BF16),
        bf=pad_lanes(b_f[l][None, :]),
        bd64=bd64,
        gfq=tile8(fox_q_norm), gfk=tile8(fox_k_norm), gdq=tile8(diff_q_norm), gdk=tile8(diff_k_norm),
        gmq=mem_q_norm[l][None, :],
        pqf=pqf, pkf=pkf, pqd=pqd, pkd=pkd,
        nmem=norm_mem[l][None, :], wmemkv=w_mem_kv[l].astype(BF16), gmk=mem_k_norm[l][None, :],
        dlam=diff_lambda[l], gdo=diff_out_norm[l][None, :],
        wpf=w_proj_fox[l].astype(BF16), wpd=w_proj_diff[l].astype(BF16), wpm=w_proj_mem[l].astype(BF16),
        wout=w_out[l].astype(BF16), nffn=norm_ffn[l][None, :], wq=w_peer_q[l].astype(BF16),
        subk=peer_subkeys[l].astype(BF16),
        u=peer_u[l].astype(BF16), vt=peer_v[l].T.astype(BF16),
        lam_init=0.8 - 0.6 * float(np.exp(-0.3 * l)),
    )


def _ffn_tail(x, ofox, od, omem, gates, wts):
    t, d = x.shape
    h, xnt, st = _merge(x, ofox, od, omem, gates, wts, _tile(t, 256))
    e1, e2, tau = _topk(st, _tile(t, 256))
    return _peer(xnt, wts["u"], wts["vt"], st, e1, e2, tau, h, _tile(t, 512), _tile(wts["u"].shape[0], 1024))


def kernel(x_prompt, x_sample, mem_prompt, cache_fox_k, cache_fox_v, cache_fox_logf, cache_diff_k, cache_diff_v, cache_mem_k, cache_mem_v, page_table, norm_mix, norm_mem, w_in, b_f, fox_q_norm, fox_k_norm, diff_q_norm, diff_k_norm, diff_lambda, diff_out_norm, w_mem_kv, mem_q_norm, mem_k_norm, w_proj_fox, w_proj_diff, w_proj_mem, w_out, norm_ffn, w_peer_q, peer_subkeys, peer_u, peer_v):
    depth = w_in.shape[0]
    b, s, d = x_prompt.shape
    db, ds, _ = x_sample.shape
    n_mem = mem_prompt.shape[1]
    n_pool = cache_fox_k.shape[1]
    w = BRANCH_WIDTH
    xp, xs = x_prompt, x_sample
    outs = [[] for _ in range(12)]
    for l in range(depth):
        wts = _layer_weights(l, d, norm_mix, norm_mem, w_in, b_f, fox_q_norm, fox_k_norm, diff_q_norm, diff_k_norm,
                             diff_lambda, diff_out_norm, w_mem_kv, mem_q_norm, mem_k_norm, w_proj_fox, w_proj_diff,
                             w_proj_mem, w_out, norm_ffn, w_peer_q, peer_subkeys, peer_u, peer_v)

        pr = _proj(xp, wts, _tile(s, 256), LOG2E)
        mk, mv = _memkv(mem_prompt.reshape(b * n_mem, d), wts["nmem"], wts["wmemkv"], wts["gmk"], _tile(b * n_mem, 256))
        tq, tk = _tile(s, FLASH_TQ), _tile(s, FLASH_TK)
        ofox = _flash(pr["qf"], pr["kf"], pr["vfb"], group=2, dv=HEAD_DIM, tq=tq, tk=tk, name="flash_fox")
        od = _flash(pr["qd"], pr["kd"], pr["vdb"], group=1, dv=2 * HEAD_DIM, tq=tq, tk=tk, name="flash_diff")
        omem = _memattn(pr["mq"], mk.reshape(b, n_mem, w), mv.reshape(b, n_mem, w), _tile(s, 512))
        xp = _ffn_tail(xp.reshape(b * s, d), ofox, od, omem.reshape(b * s, w),
                       pr["gates"].reshape(b * s, N_BRANCH * d), wts).reshape(b, s, d)
        outs[0].append(pr["fk"].reshape(b, s, FOX_HEADS, HEAD_DIM))
        outs[1].append(pr["fv"].reshape(b, s, FOX_HEADS, HEAD_DIM))
        outs[2].append(pr["logf"])
        outs[3].append(pr["dk"].reshape(b, s, 2 * DIFF_HEADS, HEAD_DIM))
        outs[4].append(pr["dv"].reshape(b, s, DIFF_HEADS, 2 * HEAD_DIM))
        outs[5].append(mk.reshape(b, n_mem, MEM_HEADS, MEM_HEAD_DIM))
        outs[6].append(mv.reshape(b, n_mem, MEM_HEADS, MEM_HEAD_DIM))

        ts = db * ds
        sr = _proj(xs.reshape(1, ts, d), wts, _tile(ts, 256), 1.0)
        pad_new = SUBLANES - ds
        pad_new_bf16 = 2 * SUBLANES - ds
        assert 2 * ds == SUBLANES, "the differential decode rows put one value head's (map, token) rows in one tile"
        nr = ds * SCORE_HEADS
        eye = jnp.eye(SCORE_HEADS, dtype=BF16)

        def block_diag_q(qp, order):
            qh = qp[0, :, :, :HEAD_DIM].reshape(SCORE_HEADS, db, ds, HEAD_DIM).transpose(1, 2, 0, 3)
            qbd = qh[:, :, :, None, :] * eye[None, None, :, :, None]
            return order(qbd).reshape(db, nr, w)

        def new_rows(a):
            return jnp.pad(a.reshape(db, ds, w), ((0, 0), (0, pad_new_bf16), (0, 0))).astype(BF16)

        def key_major(c):
            return c.transpose(0, 2, 3, 1).reshape(n_pool, w, PAGE_SIZE)

        lg = sr["logf"].reshape(db, ds, FOX_HEADS).transpose(0, 2, 1)
        npp = _tile(page_table.shape[1], PAGES_PER_STEP)
        ofs = _paged(page_table, block_diag_q(sr["qf"], lambda x: x), new_rows(sr["fk"]), new_rows(sr["fv"]), lg,
                     key_major(cache_fox_k[l]), key_major(cache_fox_v[l]), cache_fox_logf[l].transpose(0, 2, 1),
                     fox=True, pages_per_step=npp, dec_seq=ds)
        hidx = jnp.arange(FOX_HEADS)
        ofs = ofs.reshape(db, ds, FOX_HEADS, FOX_HEADS, HEAD_DIM)[:, :, hidx, hidx, :].reshape(ts, w)
        by_head = lambda x: x.reshape(db, ds, 2, DIFF_HEADS, SCORE_HEADS, HEAD_DIM).transpose(0, 3, 2, 1, 4, 5)
        dv_new = jnp.pad(sr["dv"].reshape(db, ds, DIFF_HEADS, 2 * HEAD_DIM).transpose(0, 2, 1, 3),
                         ((0, 0), (0, 0), (0, pad_new_bf16), (0, 0))).astype(BF16)
        ods = _paged(page_table, block_diag_q(sr["qd"], by_head), new_rows(sr["dk"]), dv_new, lg,
                     key_major(cache_diff_k[l]), cache_diff_v[l], None, fox=False, pages_per_step=npp, dec_seq=ds)
        ods = ods.reshape(db, DIFF_HEADS, 2, ds, 2 * HEAD_DIM).transpose(0, 3, 2, 1, 4).reshape(ts, 2 * w)
        mqs = jnp.pad(sr["mq"].reshape(db, ds, w), ((0, 0), (0, pad_new), (0, 0)))
        oms = _memattn(mqs, cache_mem_k[l], cache_mem_v[l], SUBLANES)
        oms = oms[:, :ds].reshape(ts, w)
        xs = _ffn_tail(xs.reshape(ts, d), ofs, ods, oms, sr["gates"].reshape(ts, N_BRANCH * d), wts).reshape(db, ds, d)
        outs[7].append(sr["fk"].reshape(db, ds, FOX_HEADS, HEAD_DIM))
        outs[8].append(sr["fv"].reshape(db, ds, FOX_HEADS, HEAD_DIM))
        outs[9].append(sr["logf"].reshape(db, ds, FOX_HEADS))
        outs[10].append(sr["dk"].reshape(db, ds, 2 * DIFF_HEADS, HEAD_DIM))
        outs[11].append(sr["dv"].reshape(db, ds, DIFF_HEADS, 2 * HEAD_DIM))

    return (xp, xs) + tuple(jnp.stack(o) for o in outs)
```

```python
import functools

import numpy as np
import jax
import jax.numpy as jnp
from jax import lax
from jax.experimental import pallas as pl
from jax.experimental.pallas import tpu as pltpu

F32 = jnp.float32
BF16 = jnp.bfloat16

HEAD_DIM = 64
FOX_HEADS = 8
DIFF_HEADS = 4
MEM_HEADS = 4
MEM_HEAD_DIM = 128
N_BRANCH = 3
PAGE_SIZE = 128
BRANCH_WIDTH = 512
SCORE_HEADS = 8
PEER_HEADS = 8
PEER_TOPK = 16
N_KEYS = 128
RMS_EPS = 1e-6
ATTN_SCALE = HEAD_DIM ** -0.5
MEM_SCALE = MEM_HEAD_DIM ** -0.5
LANES = 128
SUBLANES = 8
VMEM_LIMIT_BYTES = 56 * 1024 * 1024
FLASH_TQ = 1024
FLASH_TK = 1024
PAGES_PER_STEP = 32
LOG2E = 1.4426950408889634
NEG_INF = float("-inf")


def _cparams(*sem):
    return pltpu.CompilerParams(dimension_semantics=sem, vmem_limit_bytes=VMEM_LIMIT_BYTES)


def _const_spec(shape):
    nd = len(shape)
    return pl.BlockSpec(shape, lambda *_: (0,) * nd, pipeline_mode=pl.Buffered(1))


def _rms_scale(x):
    return x * lax.rsqrt(jnp.mean(x * x, axis=-1, keepdims=True) + RMS_EPS)


def _split3(x):
    hi = x.astype(BF16)
    r = x - hi.astype(F32)
    mid = r.astype(BF16)
    lo = (r - mid.astype(F32)).astype(BF16)
    return hi, mid, lo


def _dot(a, b):
    return jnp.dot(a, b, preferred_element_type=F32)


def _dot_nt(a, b):
    return lax.dot_general(a, b, (((1,), (1,)), ((), ())), preferred_element_type=F32)


def _proj_kernel(x_ref, nmix_ref, wmain_ref, wff_ref, bf_ref, bd64_ref,
                 gfq_ref, gfk_ref, gdq_ref, gdk_ref, gmq_ref,
                 pqf_ref, pkf_ref, pqd_ref, pkd_ref,
                 fk_ref, fv_ref, lf_ref, dk_ref, dv_ref, mq_ref, gates_ref,
                 qf_ref, kf_ref, qd_ref, kd_ref, vfb_ref, vdb_ref,
                 carry_ref, *, tt, d_model, gate_scale):
    i = pl.program_id(1)
    w = BRANCH_WIDTH
    x = x_ref[0]
    xn = (_rms_scale(x) * nmix_ref[...]).astype(BF16)

    def proj(c0, n):
        return _dot(xn, wmain_ref[:, c0:c0 + n])

    bd64 = bd64_ref[...]

    def headnorm64(z, g_ref):
        ms = _dot((z * z).astype(BF16), bd64)
        return z * lax.rsqrt(ms + RMS_EPS) * g_ref[...]

    fq = headnorm64(proj(0, w), gfq_ref)
    fk = headnorm64(proj(w, w), gfk_ref)
    fv = proj(2 * w, w)
    dq = headnorm64(proj(3 * w, w), gdq_ref)
    dk = headnorm64(proj(4 * w, w), gdk_ref)
    dv = proj(5 * w, w)
    zq = proj(6 * w, w)
    mq = jnp.concatenate(
        [_rms_scale(zq[:, MEM_HEAD_DIM * h:MEM_HEAD_DIM * (h + 1)]) * gmq_ref[...] for h in range(MEM_HEADS)], axis=1)
    gates = jax.nn.sigmoid(proj(7 * w, N_BRANCH * d_model))

    ff = _dot(xn, wff_ref[...]) + bf_ref[...]
    lf = jnp.minimum(ff, 0.0) - jnp.log1p(jnp.exp(-jnp.abs(ff)))

    @pl.when(i == 0)
    def _():
        carry_ref[...] = jnp.zeros_like(carry_ref)

    row = lax.broadcasted_iota(jnp.int32, (tt, tt), 0)
    col = lax.broadcasted_iota(jnp.int32, (tt, tt), 1)
    tri = jnp.where(col <= row, 1.0, 0.0).astype(BF16)
    c = _dot(tri, jnp.concatenate(_split3(lf), axis=1))
    fc = c[:, :LANES] + c[:, LANES:2 * LANES] + c[:, 2 * LANES:] + carry_ref[...]
    carry_ref[...] = fc[tt - 1:tt, :]

    lane = lax.broadcasted_iota(jnp.int32, (tt, LANES), 1)
    one_lane0 = jnp.where(lane == 0, 1.0, 0.0).astype(BF16)
    fcat = jnp.concatenate(_split3(fc * gate_scale) + (one_lane0,), axis=1)
    qb_fox = _dot(fcat, pqf_ref[...])
    kb_fox = _dot(fcat, pkf_ref[...])

    pos = i * tt + lax.broadcasted_iota(jnp.int32, (tt, LANES), 0)
    pos_hi = ((pos >> 7) << 7).astype(F32)
    pos_lo = (pos & 127).astype(F32)
    pcat = jnp.where(lane == 0, pos_hi, jnp.where(lane == 1, pos_lo, jnp.where(lane == 2, 1.0, 0.0))).astype(BF16)
    qb_diff = _dot(pcat, pqd_ref[...])
    kb_diff = _dot(pcat, pkd_ref[...])

    def pack(z, bias, out_ref, scale):
        for h in range(SCORE_HEADS):
            base = z[:, LANES * (h // 2):LANES * (h // 2 + 1)]
            if h % 2:
                base = pltpu.roll(base, HEAD_DIM, 1)
            if scale != 1.0:
                base = base * scale
            out_ref[0, h] = jnp.where(lane < HEAD_DIM, base, bias[:, LANES * h:LANES * (h + 1)]).astype(BF16)

    pack(fq, qb_fox, qf_ref, ATTN_SCALE * gate_scale)
    pack(fk, kb_fox, kf_ref, 1.0)
    pack(dq, qb_diff, qd_ref, ATTN_SCALE * gate_scale)
    pack(dk, kb_diff, kd_ref, 1.0)

    fk_ref[0] = fk
    fv_ref[0] = fv
    lf_ref[0] = lf[:, :FOX_HEADS]
    dk_ref[0] = dk
    dv_ref[0] = dv
    mq_ref[0] = mq
    gates_ref[0] = gates
    vfb_ref[0] = fv.T.astype(BF16)
    vdb_ref[0] = dv.T.astype(BF16)


def _placement_matrices():
    n = SCORE_HEADS * LANES
    pqf = np.zeros((4 * LANES, n), np.float32)
    pkf = np.zeros((4 * LANES, n), np.float32)
    pqd = np.zeros((LANES, n), np.float32)
    pkd = np.zeros((LANES, n), np.float32)
    for h in range(SCORE_HEADS):
        c = h * LANES + HEAD_DIM
        for part in range(3):
            pqf[part * LANES + h, c + part] = 1.0
            pqf[3 * LANES, c + 3 + part] = 1.0
            pkf[3 * LANES, c + part] = 1.0
            pkf[part * LANES + h, c + 3 + part] = -1.0
        coef = np.float32(2.0 ** (-8.0 * ((h % DIFF_HEADS) + 1) / DIFF_HEADS)) * np.float32(LOG2E)
        for part in range(3):
            piece = np.float32(np.asarray(coef).astype(jnp.bfloat16))
            coef = np.float32(coef - piece)
            pqd[2, c + 2 * part] = pqd[2, c + 2 * part + 1] = piece
            pkd[0, c + 2 * part] = pkd[1, c + 2 * part + 1] = 1.0
            pqd[0, c + 6 + 2 * part] = pqd[1, c + 7 + 2 * part] = -1.0
            pkd[2, c + 6 + 2 * part] = pkd[2, c + 7 + 2 * part] = piece
    return [jnp.asarray(m, BF16) for m in (pqf, pkf, pqd, pkd)]


def _proj(x, weights, tt, gate_scale):
    b, s, d = x.shape
    w = BRANCH_WIDTH
    nt = s // tt
    tok = lambda width: pl.BlockSpec((1, tt, width), lambda bi, i: (bi, i, 0))
    head = pl.BlockSpec((1, SCORE_HEADS, tt, LANES), lambda bi, i: (bi, 0, i, 0))
    consts = [weights[k] for k in ("nmix", "wmain", "wff", "bf", "bd64", "gfq", "gfk", "gdq", "gdk", "gmq",
                                   "pqf", "pkf", "pqd", "pkd")]
    names = ("fk", "fv", "logf", "dk", "dv", "mq", "gates", "qf", "kf", "qd", "kd", "vfb", "vdb")
    shapes = ([jax.ShapeDtypeStruct((b, s, w), F32)] * 2 + [jax.ShapeDtypeStruct((b, s, FOX_HEADS), F32)]
              + [jax.ShapeDtypeStruct((b, s, w), F32)] * 3 + [jax.ShapeDtypeStruct((b, s, N_BRANCH * d), F32)]
              + [jax.ShapeDtypeStruct((b, SCORE_HEADS, s, LANES), BF16)] * 4 + [jax.ShapeDtypeStruct((b, w, s), BF16)] * 2)
    specs = ([tok(w)] * 2 + [tok(FOX_HEADS)] + [tok(w)] * 3 + [tok(N_BRANCH * d)] + [head] * 4
             + [pl.BlockSpec((1, w, tt), lambda bi, i: (bi, 0, i))] * 2)
    outs = pl.pallas_call(
        functools.partial(_proj_kernel, tt=tt, d_model=d, gate_scale=gate_scale),
        grid=(b, nt),
        in_specs=[tok(d)] + [_const_spec(c.shape) for c in consts],
        out_specs=specs,
        out_shape=shapes,
        scratch_shapes=[pltpu.VMEM((1, LANES), F32)],
        compiler_params=_cparams("arbitrary", "arbitrary"),
        name="proj",
    )(x, *consts)
    return dict(zip(names, outs))


def _flash_kernel(qi_ref, kj_ref, fl_ref, q_ref, k_ref, vt_ref, o_ref, m_ref, l_ref, acc_ref, *, group, tq, tk, dv):
    p = pl.program_id(2)
    qi = qi_ref[p]
    kj = kj_ref[p]
    flags = fl_ref[p]

    @pl.when(kj == 0)
    def _():
        m_ref[...] = jnp.full_like(m_ref, NEG_INF)
        l_ref[...] = jnp.zeros_like(l_ref)
        acc_ref[...] = jnp.zeros_like(acc_ref)

    def step(masked):
        for g in range(group):
            st = _dot_nt(k_ref[0, g], q_ref[0, g])
            if masked:
                key = kj * tk + lax.broadcasted_iota(jnp.int32, st.shape, 0)
                qry = qi * tq + lax.broadcasted_iota(jnp.int32, st.shape, 1)
                st = jnp.where(key <= qry, st, NEG_INF)
            m_prev = m_ref[g]
            m_new = jnp.maximum(m_prev, jnp.max(st, axis=0, keepdims=True))
            alpha = jnp.exp2(m_prev - m_new)
            pt = jnp.exp2(st - m_new)
            l_ref[g] = alpha * l_ref[g] + jnp.sum(pt, axis=0, keepdims=True)
            acc_ref[g] = alpha * acc_ref[g] + _dot(vt_ref[0, g * dv:(g + 1) * dv, :], pt.astype(BF16))
            m_ref[g] = m_new

    @pl.when((flags & 1) == 0)
    def _():
        step(False)

    @pl.when((flags & 1) == 1)
    def _():
        step(True)

    @pl.when((flags & 2) == 2)
    def _():
        for g in range(group):
            o_ref[0, g * dv:(g + 1) * dv, :] = acc_ref[g] / l_ref[g]


def _flash(q, k, vt, *, group, dv, tq, tk, name):
    b, hs, s, _ = q.shape
    ng = hs // group
    v_blocks = vt.shape[1] // (group * dv)
    pairs = [(i, j) for i in range(s // tq) for j in range((i * tq + tq - 1) // tk + 1)]
    qi_tab = jnp.asarray([i for i, _ in pairs], jnp.int32)
    kj_tab = jnp.asarray([j for _, j in pairs], jnp.int32)
    fl_tab = jnp.asarray([((j + 1) * tk - 1 > i * tq) + 2 * (j == (i * tq + tq - 1) // tk) for i, j in pairs], jnp.int32)
    grid_spec = pltpu.PrefetchScalarGridSpec(
        num_scalar_prefetch=3,
        grid=(b, ng, len(pairs)),
        in_specs=[
            pl.BlockSpec((1, group, tq, LANES), lambda bi, g, p, qt, kt, ft: (bi, g, qt[p], 0)),
            pl.BlockSpec((1, group, tk, LANES), lambda bi, g, p, qt, kt, ft: (bi, g, kt[p], 0)),
            pl.BlockSpec((1, group * dv, tk), lambda bi, g, p, qt, kt, ft: (bi, g % v_blocks, kt[p])),
        ],
        out_specs=pl.BlockSpec((1, group * dv, tq), lambda bi, g, p, qt, kt, ft: (bi, g, qt[p])),
        scratch_shapes=[pltpu.VMEM((group, 1, tq), F32), pltpu.VMEM((group, 1, tq), F32),
                        pltpu.VMEM((group, dv, tq), F32)],
    )
    return pl.pallas_call(
        functools.partial(_flash_kernel, group=group, tq=tq, tk=tk, dv=dv),
        grid_spec=grid_spec,
        out_shape=jax.ShapeDtypeStruct((b, hs * dv, s), F32),
        compiler_params=_cparams("arbitrary", "arbitrary", "arbitrary"),
        name=name,
    )(qi_tab, kj_tab, fl_tab, q, k, vt)


def _memkv_kernel(x_ref, nmem_ref, w_ref, gk_ref, mk_ref, mv_ref):
    xn = (_rms_scale(x_ref[...]) * nmem_ref[...]).astype(BF16)
    z = _dot(xn, w_ref[...])
    mk_ref[...] = jnp.concatenate(
        [_rms_scale(z[:, MEM_HEAD_DIM * h:MEM_HEAD_DIM * (h + 1)]) * gk_ref[...] for h in range(MEM_HEADS)], axis=1)
    mv_ref[...] = z[:, BRANCH_WIDTH:]


def _memkv(mem, nmem, w_kv, gk, tm):
    t, d = mem.shape
    w = BRANCH_WIDTH
    return pl.pallas_call(
        _memkv_kernel,
        grid=(t // tm,),
        in_specs=[pl.BlockSpec((tm, d), lambda i: (i, 0)), _const_spec(nmem.shape), _const_spec(w_kv.shape),
                  _const_spec(gk.shape)],
        out_specs=[pl.BlockSpec((tm, w), lambda i: (i, 0))] * 2,
        out_shape=[jax.ShapeDtypeStruct((t, w), F32)] * 2,
        compiler_params=_cparams("arbitrary"),
        name="memkv",
    )(mem, nmem, w_kv, gk)


def _memattn_kernel(q_ref, k_ref, v_ref, o_ref):
    for b in range(q_ref.shape[0]):
        outs = []
        for h in range(MEM_HEADS):
            sl = slice(MEM_HEAD_DIM * h, MEM_HEAD_DIM * (h + 1))
            q = q_ref[b][:, sl].astype(BF16)
            if len(k_ref.shape) == 4:
                k = k_ref[b, :, h, :].astype(BF16)
                v = v_ref[b, :, h, :].astype(BF16)
            else:
                k = k_ref[b][:, sl].astype(BF16)
                v = v_ref[b][:, sl].astype(BF16)
            s = _dot_nt(q, k) * MEM_SCALE
            e = jnp.exp(s - jnp.max(s, axis=1, keepdims=True))
            outs.append(_dot(e.astype(BF16), v) / jnp.sum(e, axis=1, keepdims=True))
        o_ref[b] = jnp.concatenate(outs, axis=1)


def _memattn(q, k, v, tq, nb=1):
    bm, tm, w = q.shape
    kv_spec = pl.BlockSpec((nb,) + k.shape[1:], lambda b, i: (b,) + (0,) * (k.ndim - 1))
    return pl.pallas_call(
        _memattn_kernel,
        grid=(bm // nb, tm // tq),
        in_specs=[pl.BlockSpec((nb, tq, w), lambda b, i: (b, i, 0)), kv_spec, kv_spec],
        out_specs=pl.BlockSpec((nb, tq, w), lambda b, i: (b, i, 0)),
        out_shape=jax.ShapeDtypeStruct((bm, tm, w), F32),
        compiler_params=_cparams("arbitrary", "arbitrary"),
        name="memattn",
    )(q, k, v)


def _merge_kernel(x_ref, ofox_ref, od_ref, omem_ref, gates_ref, dlam_ref, gdo_ref,
                  wpf_ref, wpd_ref, wpm_ref, wout_ref, nffn_ref, wq_ref, subk_ref,
                  h_ref, xnt_ref, st_ref, *, d_model, lam_init, attn_t):
    w = BRANCH_WIDTH
    dl = dlam_ref[...]
    lam = (jnp.exp(jnp.sum(dl[0:1] * dl[1:2], axis=1, keepdims=True))
           - jnp.exp(jnp.sum(dl[2:3] * dl[3:4], axis=1, keepdims=True)) + lam_init)
    if attn_t:
        od = od_ref[0].T
        ofox = ofox_ref[0].T
    else:
        od = od_ref[...]
        ofox = ofox_ref[...]
    o = od[:, :w] - lam * od[:, w:]
    hw = 2 * HEAD_DIM
    odn = jnp.concatenate(
        [_rms_scale(o[:, hw * h:hw * (h + 1)]) * gdo_ref[...] * (1.0 - lam_init) for h in range(DIFF_HEADS)], axis=1)
    b_fox = _dot(ofox.astype(BF16), wpf_ref[...])
    b_diff = _dot(odn.astype(BF16), wpd_ref[...])
    b_mem = _dot(omem_ref[...].astype(BF16), wpm_ref[...])
    gates = gates_ref[...]
    m = (gates[:, :d_model] * b_fox + gates[:, d_model:2 * d_model] * b_diff + gates[:, 2 * d_model:] * b_mem)
    h = x_ref[...] + _dot(m.astype(BF16), wout_ref[...])
    h_ref[...] = h
    xn = _rms_scale(h) * nffn_ref[...]
    xnb = xn.astype(BF16)
    xnt_ref[...] = xn.T.astype(BF16)
    q = _dot(xnb, wq_ref[...])
    for hc in range(2 * PEER_HEADS):
        qhc = q[:, N_KEYS * hc:N_KEYS * (hc + 1)].astype(BF16)
        st_ref[hc] = _dot_nt(subk_ref[hc % 2], qhc)


def _merge(x, ofox, od, omem, gates, weights, tt):
    t, d = x.shape
    w = BRANCH_WIDTH
    consts = [weights[k] for k in ("dlam", "gdo", "wpf", "wpd", "wpm", "wout", "nffn", "wq", "subk")]
    rows = lambda width: pl.BlockSpec((tt, width), lambda i: (i, 0))
    attn_t = ofox.ndim == 3
    if attn_t:
        nts = ofox.shape[2] // tt
        attn = lambda width: pl.BlockSpec((1, width, tt), lambda i: (i // nts, 0, i % nts))
    else:
        attn = rows
    return pl.pallas_call(
        functools.partial(_merge_kernel, d_model=d, lam_init=weights["lam_init"], attn_t=attn_t),
        grid=(t // tt,),
        in_specs=[rows(d), attn(w), attn(2 * w), rows(w), rows(N_BRANCH * d)] + [_const_spec(c.shape) for c in consts],
        out_specs=[rows(d), pl.BlockSpec((d, tt), lambda i: (0, i)),
                   pl.BlockSpec((2 * PEER_HEADS, N_KEYS, tt), lambda i: (0, 0, i))],
        out_shape=[jax.ShapeDtypeStruct((t, d), F32), jax.ShapeDtypeStruct((d, t), BF16),
                   jax.ShapeDtypeStruct((2 * PEER_HEADS, N_KEYS, t), F32)],
        compiler_params=_cparams("arbitrary"),
        name="merge",
    )(x, ofox, od, omem, gates, *consts)


def _oddeven_merge(lo, hi, r):
    step = r * 2
    if step < hi - lo:
        yield from _oddeven_merge(lo, hi, step)
        yield from _oddeven_merge(lo + r, hi, step)
        yield from [(i, i + r) for i in range(lo + r, hi - r, step)]
    else:
        yield (lo, lo + r)


def _oddeven_merge_sort(lo, hi):
    if hi - lo >= 1:
        mid = lo + (hi - lo) // 2
        yield from _oddeven_merge_sort(lo, mid)
        yield from _oddeven_merge_sort(mid + 1, hi)
        yield from _oddeven_merge(lo, hi, 1)


_SORT16 = tuple(_oddeven_merge_sort(0, PEER_TOPK - 1))


def _cmpx(v, i, j):
    a, b = v[i], v[j]
    if b is None:
        return
    if a is None:
        v[i], v[j] = b, None
        return
    v[i], v[j] = jnp.maximum(a, b), jnp.minimum(a, b)


def _top16_replicated(v):
    v = list(v)
    for i, j in _SORT16:
        _cmpx(v, i, j)
    for shift in (4, 2, 1):
        part = [None if a is None else pltpu.roll(a, shift, 0) for a in v]
        merged = []
        for i in range(PEER_TOPK):
            a, b = v[i], part[PEER_TOPK - 1 - i]
            merged.append(b if a is None else a if b is None else jnp.maximum(a, b))
        for stride in (8, 4, 2, 1):
            for i in range(PEER_TOPK):
                if not i & stride:
                    _cmpx(merged, i, i + stride)
        v = merged
    return v


def _topk_kernel(st_ref, e1_ref, e2_ref, tau_ref, *, tl):
    sub = lax.broadcasted_iota(jnp.int32, (SUBLANES, tl), 0)

    def spread(vals):
        out = vals[SUBLANES - 1]
        for s in range(SUBLANES - 2, -1, -1):
            out = jnp.where(sub == s, vals[s], out)
        return out

    def body(h, tau_acc):
        s1 = st_ref[2 * h]
        s2 = st_ref[2 * h + 1]
        a = _top16_replicated([s1[SUBLANES * i:SUBLANES * (i + 1)] for i in range(N_KEYS // SUBLANES)])
        b = _top16_replicated([s2[SUBLANES * i:SUBLANES * (i + 1)] for i in range(N_KEYS // SUBLANES)])
        b_lo, b_hi = spread(b[:SUBLANES]), spread(b[SUBLANES:])
        cand = [a[0] + b_lo, a[0] + b_hi] + [a[i] + b_lo for i in range(1, SUBLANES)] + [spread(a[SUBLANES:]) + b[0]]
        best = _top16_replicated(cand + [None] * (PEER_TOPK - len(cand)))
        z = jnp.exp(best[0] - best[0])
        for kk in range(1, PEER_TOPK):
            z = z + jnp.exp(best[kk] - best[0])
        reps = N_KEYS // SUBLANES
        e1_ref[h] = jnp.exp(s1 - jnp.concatenate([a[0]] * reps, axis=0))
        e2_ref[h] = jnp.exp(s2 - jnp.concatenate([b[0]] * reps, axis=0)) / jnp.concatenate([z] * reps, axis=0)
        return jnp.where(sub == h, best[PEER_TOPK - 1], tau_acc)

    tau_ref[...] = lax.fori_loop(0, PEER_HEADS, body, jnp.zeros((SUBLANES, tl), F32))


def _topk(st, tl):
    nhc, nk, t = st.shape
    blk = lambda n: pl.BlockSpec((n, nk, tl), lambda i: (0, 0, i))
    return pl.pallas_call(
        functools.partial(_topk_kernel, tl=tl),
        grid=(t // tl,),
        in_specs=[blk(nhc)],
        out_specs=[blk(PEER_HEADS), blk(PEER_HEADS), pl.BlockSpec((PEER_HEADS, tl), lambda i: (0, i))],
        out_shape=[jax.ShapeDtypeStruct((PEER_HEADS, nk, t), F32)] * 2 + [jax.ShapeDtypeStruct((PEER_HEADS, t), F32)],
        compiler_params=_cparams("arbitrary"),
        name="topk",
    )(st)


def _peer_kernel(xt_ref, u_ref, vt_ref, st_ref, e1_ref, e2_ref, tau_ref, h_ref, y_ref, acc_ref, *, eb, tt):
    ej = pl.program_id(1)

    @pl.when(ej == 0)
    def _():
        acc_ref[...] = jnp.zeros_like(acc_ref)

    ht = _dot(u_ref[...], xt_ref[...])
    acts = []
    for j in range(eb // N_KEYS):
        a = ej * (eb // N_KEYS) + j
        wt = jnp.zeros((N_KEYS, tt), F32)
        for h in range(PEER_HEADS):
            s = st_ref[2 * h, pl.ds(a, 1), :] + st_ref[2 * h + 1]
            val = e1_ref[h, pl.ds(a, 1), :] * e2_ref[h]
            wt = wt + jnp.where(s >= tau_ref[h:h + 1, :], val, 0.0)
        hj = ht[N_KEYS * j:N_KEYS * (j + 1)]
        gelu = 0.5 * hj * (1.0 + lax.erf(hj * (2.0 ** -0.5)))
        acts.append((wt * gelu).astype(BF16))
    acc_ref[...] += _dot(vt_ref[...], jnp.concatenate(acts, axis=0))

    @pl.when(ej == pl.num_programs(1) - 1)
    def _():
        y_ref[...] = h_ref[...] + acc_ref[...].T


def _peer(xt, u, vt, st, e1, e2, tau, h, tt, eb):
    d, t = xt.shape
    ne = u.shape[0]
    tok3 = lambda n: pl.BlockSpec((n, N_KEYS, tt), lambda i, j: (0, 0, i))
    return pl.pallas_call(
        functools.partial(_peer_kernel, eb=eb, tt=tt),
        grid=(t // tt, ne // eb),
        in_specs=[pl.BlockSpec((d, tt), lambda i, j: (0, i)),
                  pl.BlockSpec((eb, d), lambda i, j: (j, 0)),
                  pl.BlockSpec((d, eb), lambda i, j: (0, j)),
                  tok3(2 * PEER_HEADS), tok3(PEER_HEADS), tok3(PEER_HEADS),
                  pl.BlockSpec((PEER_HEADS, tt), lambda i, j: (0, i)),
                  pl.BlockSpec((tt, d), lambda i, j: (i, 0))],
        out_specs=pl.BlockSpec((tt, d), lambda i, j: (i, 0)),
        out_shape=jax.ShapeDtypeStruct((t, d), F32),
        scratch_shapes=[pltpu.VMEM((d, tt), F32)],
        compiler_params=_cparams("arbitrary", "arbitrary"),
        name="peer",
    )(xt, u, vt, st, e1, e2, tau, h)


def _paged_kernel(pt_ref, q_ref, knew_ref, vnew_ref, lg_ref, *rest, fox, n_pages, pages_per_step, dec_seq, past_len):
    npp = pages_per_step
    if not fox:
        spread_ref, rest = rest[0], rest[1:]
    k_refs = rest[:npp]
    v_refs = rest[npp:2 * npp]
    rest = rest[2 * npp:]
    lf_refs = rest[:npp] if fox else ()
    o_ref, m_ref, l_ref, acc_ref, carry_ref = rest[-5:]
    j = pl.program_id(1)
    rows = dec_seq * SCORE_HEADS
    q = q_ref[0]
    r_iota = lax.broadcasted_iota(jnp.int32, (rows, 1), 0)
    if fox:
        r_t = r_iota // SCORE_HEADS
        lg = jnp.concatenate([lg_ref[0]] * dec_seq, axis=0)
        n_row = jnp.zeros((rows, 1), F32)
        for u in range(dec_seq):
            n_row = n_row + jnp.where(r_t >= u, lg[:, u:u + 1], 0.0)
    else:
        r_t = r_iota % dec_seq
        r_h = r_iota // (2 * dec_seq)
        slope = jnp.where(r_h == 0, 2.0 ** -2, jnp.where(r_h == 1, 2.0 ** -4, jnp.where(r_h == 2, 2.0 ** -6, 2.0 ** -8)))

    def head_rows(p, h):
        return p[SUBLANES * h:SUBLANES * (h + 1)]

    def update(scores, weigh):
        m_prev = m_ref[...]
        m_new = m_prev
        for s in scores:
            m_new = jnp.maximum(m_new, jnp.max(s, axis=1, keepdims=True))
        alpha = jnp.exp(m_prev - m_new)
        l_new = alpha * l_ref[...]
        acc = alpha * acc_ref[...]
        for s, pv in zip(scores, weigh):
            pexp = jnp.exp(s - m_new)
            l_new = l_new + jnp.sum(pexp, axis=1, keepdims=True)
            acc = acc + pv(pexp.astype(BF16))
        m_ref[...] = m_new
        l_ref[...] = l_new
        acc_ref[...] = acc

    @pl.when(j == 0)
    def _():
        m_ref[...] = jnp.full_like(m_ref, NEG_INF)
        l_ref[...] = jnp.zeros_like(l_ref)
        acc_ref[...] = jnp.zeros_like(acc_ref)
        carry_ref[...] = jnp.zeros_like(carry_ref)
        s = _dot_nt(q, knew_ref[0])
        col = lax.broadcasted_iota(jnp.int32, (1, s.shape[1]), 1)
        if fox:
            n_col = jnp.zeros(s.shape, F32)
            for u in range(dec_seq):
                n_col = n_col + jnp.where(col >= u, lg[:, u:u + 1], 0.0)
            s = s + (n_row - n_col)
            pv = lambda p: _dot(p, vnew_ref[0])
        else:
            s = s - slope * (r_t - col).astype(F32)
            pv = lambda p: jnp.concatenate([_dot(head_rows(p, h), vnew_ref[0, h]) for h in range(DIFF_HEADS)], axis=0)
        update([jnp.where(col <= r_t, s, NEG_INF)], [pv])

    @pl.when(j > 0)
    def _():
        tok = lax.broadcasted_iota(jnp.int32, (1, PAGE_SIZE), 1)
        if fox:
            ri = lax.broadcasted_iota(jnp.int32, (PAGE_SIZE, PAGE_SIZE), 0)
            ci = lax.broadcasted_iota(jnp.int32, (PAGE_SIZE, PAGE_SIZE), 1)
            later = jnp.where(ri > ci, 1.0, 0.0).astype(BF16)
        else:
            vcol = lax.broadcasted_iota(jnp.int32, (1, PAGE_SIZE * DIFF_HEADS), 1)
            own_head = (vcol % DIFF_HEADS) == r_h
        scores, weigh = [], []
        for i in range(npp):
            slot = n_pages - 1 - ((j - 1) * npp + i)
            s = _dot(q, k_refs[i][0].astype(BF16))
            if fox:
                lf = lf_refs[i][0]
                c = _dot(jnp.concatenate(_split3(lf), axis=0), later)
                after = c[:FOX_HEADS] + c[FOX_HEADS:2 * FOX_HEADS] + c[2 * FOX_HEADS:] + carry_ref[...]
                carry_ref[...] = after[:, 0:1] + lf[:, 0:1]
                s = s + (jnp.concatenate([after] * dec_seq, axis=0) + n_row)
                weigh.append(lambda p, v_ref=v_refs[i]: _dot_nt(p, v_ref[0].astype(BF16)))
            else:
                s = s - slope * (past_len + r_t - slot * PAGE_SIZE - tok).astype(F32)
                def pv(p, v_ref=v_refs[i]):
                    v = v_ref[0].reshape(PAGE_SIZE * DIFF_HEADS, 2 * HEAD_DIM).astype(BF16)
                    return _dot(jnp.where(own_head, _dot(p, spread_ref[...]), 0.0).astype(BF16), v)
                weigh.append(pv)
            scores.append(s)
        update(scores, weigh)

    @pl.when(j == pl.num_programs(1) - 1)
    def _():
        o_ref[0] = acc_ref[...] / l_ref[...]


def _paged(page_table, q, knew, vnew, lg, kt_pool, v_pool, lf_pool, *, fox, pages_per_step, dec_seq):
    b, n_pages = page_table.shape
    npp = pages_per_step
    rows = q.shape[1]
    dv = BRANCH_WIDTH if fox else 2 * HEAD_DIM
    pt = page_table.reshape(-1)

    def page_map(i, nd):
        def index(bi, j, pt_ref):
            slot = n_pages - 1 - (jnp.maximum(j, 1) - 1) * npp - i
            return (pt_ref[bi * n_pages + slot],) + (0,) * nd
        return index

    def per_seq(a):
        nd = a.ndim - 1
        return pl.BlockSpec((1,) + a.shape[1:], lambda bi, j, pt_ref: (bi,) + (0,) * nd)

    def paged(a):
        return [pl.BlockSpec((1,) + a.shape[1:], page_map(i, a.ndim - 1)) for i in range(npp)]

    in_specs = [per_seq(a) for a in (q, knew, vnew, lg)]
    args = [q, knew, vnew, lg]
    if not fox:
        tok = np.arange(PAGE_SIZE)[:, None]
        col = np.arange(PAGE_SIZE * DIFF_HEADS)[None, :]
        spread = jnp.asarray(col // DIFF_HEADS == tok, BF16)
        in_specs.append(pl.BlockSpec(spread.shape, lambda bi, j, pt_ref: (0, 0), pipeline_mode=pl.Buffered(1)))
        args.append(spread)
    in_specs += paged(kt_pool) + paged(v_pool)
    args += [kt_pool] * npp + [v_pool] * npp
    if fox:
        in_specs += paged(lf_pool)
        args += [lf_pool] * npp
    grid_spec = pltpu.PrefetchScalarGridSpec(
        num_scalar_prefetch=1,
        grid=(b, n_pages // npp + 1),
        in_specs=in_specs,
        out_specs=pl.BlockSpec((1, rows, dv), lambda bi, j, pt_ref: (bi, 0, 0)),
        scratch_shapes=[pltpu.VMEM((rows, 1), F32), pltpu.VMEM((rows, 1), F32), pltpu.VMEM((rows, dv), F32),
                        pltpu.VMEM((SCORE_HEADS, 1), F32)],
    )
    return pl.pallas_call(
        functools.partial(_paged_kernel, fox=fox, n_pages=n_pages, pages_per_step=npp, dec_seq=dec_seq,
                          past_len=n_pages * PAGE_SIZE),
        grid_spec=grid_spec,
        out_shape=jax.ShapeDtypeStruct((b, rows, dv), F32),
        compiler_params=_cparams("arbitrary", "arbitrary"),
        name="paged_fox" if fox else "paged_diff",
    )(pt, *args)


def _tile(n, target):
    t = min(n, target)
    while n % t:
        t -= 1
    return t


def _layer_weights(layer, d, norm_mix, norm_mem, w_in, b_f, fox_q_norm, fox_k_norm, diff_q_norm, diff_k_norm,
                   diff_lambda, diff_out_norm, w_mem_kv, mem_q_norm, mem_k_norm, w_proj_fox, w_proj_diff,
                   w_proj_mem, w_out, norm_ffn, w_peer_q, peer_subkeys, peer_u, peer_v):
    w = BRANCH_WIDTH
    l = layer
    win = w_in[l]
    ff0 = 3 * w
    pad_lanes = lambda a: jnp.pad(a, ((0, 0), (0, LANES - a.shape[1])))
    pqf, pkf, pqd, pkd = _placement_matrices()
    bd64 = jnp.asarray(np.kron(np.eye(SCORE_HEADS), np.full((HEAD_DIM, HEAD_DIM), 1.0 / HEAD_DIM)), BF16)
    tile8 = lambda g: jnp.tile(g[l], SCORE_HEADS)[None, :]
    return dict(
        nmix=norm_mix[l][None, :],
        wmain=jnp.concatenate([win[:, :ff0], win[:, ff0 + FOX_HEADS:]], axis=1).astype(BF16),
        wff=pad_lanes(win[:, ff0:ff0 + FOX_HEADS]).astype(BF16),
        bf=pad_lanes(b_f[l][None, :]),
        bd64=bd64,
        gfq=tile8(fox_q_norm), gfk=tile8(fox_k_norm), gdq=tile8(diff_q_norm), gdk=tile8(diff_k_norm),
        gmq=mem_q_norm[l][None, :],
        pqf=pqf, pkf=pkf, pqd=pqd, pkd=pkd,
        nmem=norm_mem[l][None, :], wmemkv=w_mem_kv[l].astype(BF16), gmk=mem_k_norm[l][None, :],
        dlam=diff_lambda[l], gdo=diff_out_norm[l][None, :],
        wpf=w_proj_fox[l].astype(BF16), wpd=w_proj_diff[l].astype(BF16), wpm=w_proj_mem[l].astype(BF16),
        wout=w_out[l].astype(BF16), nffn=norm_ffn[l][None, :], wq=w_peer_q[l].astype(BF16),
        subk=peer_subkeys[l].astype(BF16),
        u=peer_u[l].astype(BF16), vt=peer_v[l].T.astype(BF16),
        lam_init=0.8 - 0.6 * float(np.exp(-0.3 * l)),
    )


def _ffn_tail(x, ofox, od, omem, gates, wts):
    t, d = x.shape
    h, xnt, st = _merge(x, ofox, od, omem, gates, wts, _tile(t, 256))
    e1, e2, tau = _topk(st, _tile(t, 256))
    return _peer(xnt, wts["u"], wts["vt"], st, e1, e2, tau, h, _tile(t, 512), _tile(wts["u"].shape[0], 1024))


def kernel(x_prompt, x_sample, mem_prompt, cache_fox_k, cache_fox_v, cache_fox_logf, cache_diff_k, cache_diff_v, cache_mem_k, cache_mem_v, page_table, norm_mix, norm_mem, w_in, b_f, fox_q_norm, fox_k_norm, diff_q_norm, diff_k_norm, diff_lambda, diff_out_norm, w_mem_kv, mem_q_norm, mem_k_norm, w_proj_fox, w_proj_diff, w_proj_mem, w_out, norm_ffn, w_peer_q, peer_subkeys, peer_u, peer_v):
    depth = w_in.shape[0]
    b, s, d = x_prompt.shape
    db, ds, _ = x_sample.shape
    n_mem = mem_prompt.shape[1]
    n_pool = cache_fox_k.shape[1]
    w = BRANCH_WIDTH
    xp, xs = x_prompt, x_sample
    outs = [[] for _ in range(12)]
    for l in range(depth):
        wts = _layer_weights(l, d, norm_mix, norm_mem, w_in, b_f, fox_q_norm, fox_k_norm, diff_q_norm, diff_k_norm,
                             diff_lambda, diff_out_norm, w_mem_kv, mem_q_norm, mem_k_norm, w_proj_fox, w_proj_diff,
                             w_proj_mem, w_out, norm_ffn, w_peer_q, peer_subkeys, peer_u, peer_v)

        pr = _proj(xp, wts, _tile(s, 256), LOG2E)
        mk, mv = _memkv(mem_prompt.reshape(b * n_mem, d), wts["nmem"], wts["wmemkv"], wts["gmk"], _tile(b * n_mem, 256))
        tq, tk = _tile(s, FLASH_TQ), _tile(s, FLASH_TK)
        ofox = _flash(pr["qf"], pr["kf"], pr["vfb"], group=2, dv=HEAD_DIM, tq=tq, tk=tk, name="flash_fox")
        od = _flash(pr["qd"], pr["kd"], pr["vdb"], group=2, dv=2 * HEAD_DIM, tq=tq, tk=tk, name="flash_diff")
        omem = _memattn(pr["mq"], mk.reshape(b, n_mem, w), mv.reshape(b, n_mem, w), _tile(s, 512))
        xp = _ffn_tail(xp.reshape(b * s, d), ofox, od, omem.reshape(b * s, w),
                       pr["gates"].reshape(b * s, N_BRANCH * d), wts).reshape(b, s, d)
        outs[0].append(pr["fk"].reshape(b, s, FOX_HEADS, HEAD_DIM))
        outs[1].append(pr["fv"].reshape(b, s, FOX_HEADS, HEAD_DIM))
        outs[2].append(pr["logf"])
        outs[3].append(pr["dk"].reshape(b, s, 2 * DIFF_HEADS, HEAD_DIM))
        outs[4].append(pr["dv"].reshape(b, s, DIFF_HEADS, 2 * HEAD_DIM))
        outs[5].append(mk.reshape(b, n_mem, MEM_HEADS, MEM_HEAD_DIM))
        outs[6].append(mv.reshape(b, n_mem, MEM_HEADS, MEM_HEAD_DIM))

        ts = db * ds
        sr = _proj(xs.reshape(1, ts, d), wts, _tile(ts, 256), 1.0)
        pad_new = SUBLANES - ds
        pad_new_bf16 = 2 * SUBLANES - ds
        assert 2 * ds == SUBLANES, "the differential decode rows put one value head's (map, token) rows in one tile"
        nr = ds * SCORE_HEADS
        eye = jnp.eye(SCORE_HEADS, dtype=BF16)

        def block_diag_q(qp, order):
            qh = qp[0, :, :, :HEAD_DIM].reshape(SCORE_HEADS, db, ds, HEAD_DIM).transpose(1, 2, 0, 3)
            qbd = qh[:, :, :, None, :] * eye[None, None, :, :, None]
            return order(qbd).reshape(db, nr, w)

        def new_rows(a):
            return jnp.pad(a.reshape(db, ds, w), ((0, 0), (0, pad_new_bf16), (0, 0))).astype(BF16)

        def key_major(c):
            return c.transpose(0, 2, 3, 1).reshape(n_pool, w, PAGE_SIZE)

        lg = sr["logf"].reshape(db, ds, FOX_HEADS).transpose(0, 2, 1)
        npp = _tile(page_table.shape[1], PAGES_PER_STEP)
        ofs = _paged(page_table, block_diag_q(sr["qf"], lambda x: x), new_rows(sr["fk"]), new_rows(sr["fv"]), lg,
                     key_major(cache_fox_k[l]), key_major(cache_fox_v[l]), cache_fox_logf[l].transpose(0, 2, 1),
                     fox=True, pages_per_step=npp, dec_seq=ds)
        hidx = jnp.arange(FOX_HEADS)
        ofs = ofs.reshape(db, ds, FOX_HEADS, FOX_HEADS, HEAD_DIM)[:, :, hidx, hidx, :].reshape(ts, w)
        by_head = lambda x: x.reshape(db, ds, 2, DIFF_HEADS, SCORE_HEADS, HEAD_DIM).transpose(0, 3, 2, 1, 4, 5)
        dv_new = jnp.pad(sr["dv"].reshape(db, ds, DIFF_HEADS, 2 * HEAD_DIM).transpose(0, 2, 1, 3),
                         ((0, 0), (0, 0), (0, pad_new_bf16), (0, 0))).astype(BF16)
        ods = _paged(page_table, block_diag_q(sr["qd"], by_head), new_rows(sr["dk"]), dv_new, lg,
                     key_major(cache_diff_k[l]), cache_diff_v[l], None, fox=False, pages_per_step=npp, dec_seq=ds)
        ods = ods.reshape(db, DIFF_HEADS, 2, ds, 2 * HEAD_DIM).transpose(0, 3, 2, 1, 4).reshape(ts, 2 * w)
        mqs = jnp.pad(sr["mq"].reshape(db, ds, w), ((0, 0), (0, pad_new), (0, 0)))
        oms = _memattn(mqs, cache_mem_k[l], cache_mem_v[l], SUBLANES, _tile(db, 8))
        oms = oms[:, :ds].reshape(ts, w)
        xs = _ffn_tail(xs.reshape(ts, d), ofs, ods, oms, sr["gates"].reshape(ts, N_BRANCH * d), wts).reshape(db, ds, d)
        outs[7].append(sr["fk"].reshape(db, ds, FOX_HEADS, HEAD_DIM))
        outs[8].append(sr["fv"].reshape(db, ds, FOX_HEADS, HEAD_DIM))
        outs[9].append(sr["logf"].reshape(db, ds, FOX_HEADS))
        outs[10].append(sr["dk"].reshape(db, ds, 2 * DIFF_HEADS, HEAD_DIM))
        outs[11].append(sr["dv"].reshape(db, ds, DIFF_HEADS, 2 * HEAD_DIM))

    return (xp, xs) + tuple(jnp.stack(o) for o in outs)
```

```python
import functools

import numpy as np
import jax
import jax.numpy as jnp
from jax import lax
from jax.experimental import pallas as pl
from jax.experimental.pallas import tpu as pltpu

F32 = jnp.float32
BF16 = jnp.bfloat16

HEAD_DIM = 64
FOX_HEADS = 8
DIFF_HEADS = 4
MEM_HEADS = 4
MEM_HEAD_DIM = 128
N_BRANCH = 3
PAGE_SIZE = 128
BRANCH_WIDTH = 512
SCORE_HEADS = 8
PEER_HEADS = 8
PEER_TOPK = 16
N_KEYS = 128
RMS_EPS = 1e-6
ATTN_SCALE = HEAD_DIM ** -0.5
MEM_SCALE = MEM_HEAD_DIM ** -0.5
LANES = 128
SUBLANES = 8
VMEM_LIMIT_BYTES = 56 * 1024 * 1024
FLASH_TQ = 1024
FLASH_TK = 1024
PAGES_PER_STEP = 32
LOG2E = 1.4426950408889634
NEG_INF = float("-inf")


def _cparams(*sem):
    return pltpu.CompilerParams(dimension_semantics=sem, vmem_limit_bytes=VMEM_LIMIT_BYTES)


def _const_spec(shape):
    nd = len(shape)
    return pl.BlockSpec(shape, lambda *_: (0,) * nd, pipeline_mode=pl.Buffered(1))


def _rms_scale(x):
    return x * lax.rsqrt(jnp.mean(x * x, axis=-1, keepdims=True) + RMS_EPS)


def _split3(x):
    hi = x.astype(BF16)
    r = x - hi.astype(F32)
    mid = r.astype(BF16)
    lo = (r - mid.astype(F32)).astype(BF16)
    return hi, mid, lo


def _dot(a, b):
    return jnp.dot(a, b, preferred_element_type=F32)


def _dot_nt(a, b):
    return lax.dot_general(a, b, (((1,), (1,)), ((), ())), preferred_element_type=F32)


def _proj_kernel(x_ref, nmix_ref, wmain_ref, wff_ref, bf_ref, bd64_ref,
                 gfq_ref, gfk_ref, gdq_ref, gdk_ref, gmq_ref,
                 pqf_ref, pkf_ref, pqd_ref, pkd_ref,
                 fk_ref, fv_ref, lf_ref, dk_ref, dv_ref, mq_ref, gates_ref,
                 qf_ref, kf_ref, qd_ref, kd_ref, vfb_ref, vdb_ref,
                 carry_ref, *, tt, d_model, gate_scale):
    i = pl.program_id(1)
    w = BRANCH_WIDTH
    x = x_ref[0]
    xn = (_rms_scale(x) * nmix_ref[...]).astype(BF16)

    def proj(c0, n):
        return _dot(xn, wmain_ref[:, c0:c0 + n])

    bd64 = bd64_ref[...]

    def headnorm64(z, g_ref):
        ms = _dot((z * z).astype(BF16), bd64)
        return z * lax.rsqrt(ms + RMS_EPS) * g_ref[...]

    fq = headnorm64(proj(0, w), gfq_ref)
    fk = headnorm64(proj(w, w), gfk_ref)
    fv = proj(2 * w, w)
    dq = headnorm64(proj(3 * w, w), gdq_ref)
    dk = headnorm64(proj(4 * w, w), gdk_ref)
    dv = proj(5 * w, w)
    zq = proj(6 * w, w)
    mq = jnp.concatenate(
        [_rms_scale(zq[:, MEM_HEAD_DIM * h:MEM_HEAD_DIM * (h + 1)]) * gmq_ref[...] for h in range(MEM_HEADS)], axis=1)
    gates = jax.nn.sigmoid(proj(7 * w, N_BRANCH * d_model))

    ff = _dot(xn, wff_ref[...]) + bf_ref[...]
    lf = jnp.minimum(ff, 0.0) - jnp.log1p(jnp.exp(-jnp.abs(ff)))

    @pl.when(i == 0)
    def _():
        carry_ref[...] = jnp.zeros_like(carry_ref)

    row = lax.broadcasted_iota(jnp.int32, (tt, tt), 0)
    col = lax.broadcasted_iota(jnp.int32, (tt, tt), 1)
    tri = jnp.where(col <= row, 1.0, 0.0).astype(BF16)
    c = _dot(tri, jnp.concatenate(_split3(lf), axis=1))
    fc = c[:, :LANES] + c[:, LANES:2 * LANES] + c[:, 2 * LANES:] + carry_ref[...]
    carry_ref[...] = fc[tt - 1:tt, :]

    lane = lax.broadcasted_iota(jnp.int32, (tt, LANES), 1)
    one_lane0 = jnp.where(lane == 0, 1.0, 0.0).astype(BF16)
    fcat = jnp.concatenate(_split3(fc * gate_scale) + (one_lane0,), axis=1)
    qb_fox = _dot(fcat, pqf_ref[...])
    kb_fox = _dot(fcat, pkf_ref[...])

    pos = i * tt + lax.broadcasted_iota(jnp.int32, (tt, LANES), 0)
    pos_hi = ((pos >> 7) << 7).astype(F32)
    pos_lo = (pos & 127).astype(F32)
    pcat = jnp.where(lane == 0, pos_hi, jnp.where(lane == 1, pos_lo, jnp.where(lane == 2, 1.0, 0.0))).astype(BF16)
    qb_diff = _dot(pcat, pqd_ref[...])
    kb_diff = _dot(pcat, pkd_ref[...])

    def pack(z, bias, out_ref, scale):
        for h in range(SCORE_HEADS):
            base = z[:, LANES * (h // 2):LANES * (h // 2 + 1)]
            if h % 2:
                base = pltpu.roll(base, HEAD_DIM, 1)
            if scale != 1.0:
                base = base * scale
            out_ref[0, h] = jnp.where(lane < HEAD_DIM, base, bias[:, LANES * h:LANES * (h + 1)]).astype(BF16)

    pack(fq, qb_fox, qf_ref, ATTN_SCALE * gate_scale)
    pack(fk, kb_fox, kf_ref, 1.0)
    pack(dq, qb_diff, qd_ref, ATTN_SCALE * gate_scale)
    pack(dk, kb_diff, kd_ref, 1.0)

    fk_ref[0] = fk
    fv_ref[0] = fv
    lf_ref[0] = lf[:, :FOX_HEADS]
    dk_ref[0] = dk
    dv_ref[0] = dv
    mq_ref[0] = mq
    gates_ref[0] = gates
    vfb_ref[0] = fv.T.astype(BF16)
    vdb_ref[0] = dv.T.astype(BF16)


def _placement_matrices():
    n = SCORE_HEADS * LANES
    pqf = np.zeros((4 * LANES, n), np.float32)
    pkf = np.zeros((4 * LANES, n), np.float32)
    pqd = np.zeros((LANES, n), np.float32)
    pkd = np.zeros((LANES, n), np.float32)
    for h in range(SCORE_HEADS):
        c = h * LANES + HEAD_DIM
        for part in range(3):
            pqf[part * LANES + h, c + part] = 1.0
            pqf[3 * LANES, c + 3 + part] = 1.0
            pkf[3 * LANES, c + part] = 1.0
            pkf[part * LANES + h, c + 3 + part] = -1.0
        coef = np.float32(2.0 ** (-8.0 * ((h % DIFF_HEADS) + 1) / DIFF_HEADS)) * np.float32(LOG2E)
        for part in range(3):
            piece = np.float32(np.asarray(coef).astype(jnp.bfloat16))
            coef = np.float32(coef - piece)
            pqd[2, c + 2 * part] = pqd[2, c + 2 * part + 1] = piece
            pkd[0, c + 2 * part] = pkd[1, c + 2 * part + 1] = 1.0
            pqd[0, c + 6 + 2 * part] = pqd[1, c + 7 + 2 * part] = -1.0
            pkd[2, c + 6 + 2 * part] = pkd[2, c + 7 + 2 * part] = piece
    return [jnp.asarray(m, BF16) for m in (pqf, pkf, pqd, pkd)]


def _proj(x, weights, tt, gate_scale):
    b, s, d = x.shape
    w = BRANCH_WIDTH
    nt = s // tt
    tok = lambda width: pl.BlockSpec((1, tt, width), lambda bi, i: (bi, i, 0))
    head = pl.BlockSpec((1, SCORE_HEADS, tt, LANES), lambda bi, i: (bi, 0, i, 0))
    consts = [weights[k] for k in ("nmix", "wmain", "wff", "bf", "bd64", "gfq", "gfk", "gdq", "gdk", "gmq",
                                   "pqf", "pkf", "pqd", "pkd")]
    names = ("fk", "fv", "logf", "dk", "dv", "mq", "gates", "qf", "kf", "qd", "kd", "vfb", "vdb")
    shapes = ([jax.ShapeDtypeStruct((b, s, w), F32)] * 2 + [jax.ShapeDtypeStruct((b, s, FOX_HEADS), F32)]
              + [jax.ShapeDtypeStruct((b, s, w), F32)] * 3 + [jax.ShapeDtypeStruct((b, s, N_BRANCH * d), F32)]
              + [jax.ShapeDtypeStruct((b, SCORE_HEADS, s, LANES), BF16)] * 4 + [jax.ShapeDtypeStruct((b, w, s), BF16)] * 2)
    specs = ([tok(w)] * 2 + [tok(FOX_HEADS)] + [tok(w)] * 3 + [tok(N_BRANCH * d)] + [head] * 4
             + [pl.BlockSpec((1, w, tt), lambda bi, i: (bi, 0, i))] * 2)
    outs = pl.pallas_call(
        functools.partial(_proj_kernel, tt=tt, d_model=d, gate_scale=gate_scale),
        grid=(b, nt),
        in_specs=[tok(d)] + [_const_spec(c.shape) for c in consts],
        out_specs=specs,
        out_shape=shapes,
        scratch_shapes=[pltpu.VMEM((1, LANES), F32)],
        compiler_params=_cparams("arbitrary", "arbitrary"),
        name="proj",
    )(x, *consts)
    return dict(zip(names, outs))


def _flash_kernel(qi_ref, kj_ref, fl_ref, q_ref, k_ref, vt_ref, o_ref, m_ref, l_ref, acc_ref, *, group, tq, tk, dv):
    p = pl.program_id(2)
    qi = qi_ref[p]
    kj = kj_ref[p]
    flags = fl_ref[p]

    @pl.when(kj == 0)
    def _():
        m_ref[...] = jnp.full_like(m_ref, NEG_INF)
        l_ref[...] = jnp.zeros_like(l_ref)
        acc_ref[...] = jnp.zeros_like(acc_ref)

    def step(masked):
        for g in range(group):
            st = _dot_nt(k_ref[0, g], q_ref[0, g])
            if masked:
                key = kj * tk + lax.broadcasted_iota(jnp.int32, st.shape, 0)
                qry = qi * tq + lax.broadcasted_iota(jnp.int32, st.shape, 1)
                st = jnp.where(key <= qry, st, NEG_INF)
            m_prev = m_ref[g]
            m_new = jnp.maximum(m_prev, jnp.max(st, axis=0, keepdims=True))
            alpha = jnp.exp2(m_prev - m_new)
            pt = jnp.exp2(st - m_new)
            l_ref[g] = alpha * l_ref[g] + jnp.sum(pt, axis=0, keepdims=True)
            acc_ref[g] = alpha * acc_ref[g] + _dot(vt_ref[0, g * dv:(g + 1) * dv, :], pt.astype(BF16))
            m_ref[g] = m_new

    @pl.when((flags & 1) == 0)
    def _():
        step(False)

    @pl.when((flags & 1) == 1)
    def _():
        step(True)

    @pl.when((flags & 2) == 2)
    def _():
        for g in range(group):
            o_ref[0, g * dv:(g + 1) * dv, :] = acc_ref[g] / l_ref[g]


def _flash(q, k, vt, *, group, dv, tq, tk, name):
    b, hs, s, _ = q.shape
    ng = hs // group
    v_blocks = vt.shape[1] // (group * dv)
    pairs = [(i, j) for i in range(s // tq) for j in range((i * tq + tq - 1) // tk + 1)]
    qi_tab = jnp.asarray([i for i, _ in pairs], jnp.int32)
    kj_tab = jnp.asarray([j for _, j in pairs], jnp.int32)
    fl_tab = jnp.asarray([((j + 1) * tk - 1 > i * tq) + 2 * (j == (i * tq + tq - 1) // tk) for i, j in pairs], jnp.int32)
    grid_spec = pltpu.PrefetchScalarGridSpec(
        num_scalar_prefetch=3,
        grid=(b, ng, len(pairs)),
        in_specs=[
            pl.BlockSpec((1, group, tq, LANES), lambda bi, g, p, qt, kt, ft: (bi, g, qt[p], 0)),
            pl.BlockSpec((1, group, tk, LANES), lambda bi, g, p, qt, kt, ft: (bi, g, kt[p], 0)),
            pl.BlockSpec((1, group * dv, tk), lambda bi, g, p, qt, kt, ft: (bi, g % v_blocks, kt[p])),
        ],
        out_specs=pl.BlockSpec((1, group * dv, tq), lambda bi, g, p, qt, kt, ft: (bi, g, qt[p])),
        scratch_shapes=[pltpu.VMEM((group, 1, tq), F32), pltpu.VMEM((group, 1, tq), F32),
                        pltpu.VMEM((group, dv, tq), F32)],
    )
    return pl.pallas_call(
        functools.partial(_flash_kernel, group=group, tq=tq, tk=tk, dv=dv),
        grid_spec=grid_spec,
        out_shape=jax.ShapeDtypeStruct((b, hs * dv, s), F32),
        compiler_params=_cparams("arbitrary", "arbitrary", "arbitrary"),
        name=name,
    )(qi_tab, kj_tab, fl_tab, q, k, vt)


def _memkv_kernel(x_ref, nmem_ref, w_ref, gk_ref, mk_ref, mv_ref):
    xn = (_rms_scale(x_ref[...]) * nmem_ref[...]).astype(BF16)
    z = _dot(xn, w_ref[...])
    mk_ref[...] = jnp.concatenate(
        [_rms_scale(z[:, MEM_HEAD_DIM * h:MEM_HEAD_DIM * (h + 1)]) * gk_ref[...] for h in range(MEM_HEADS)], axis=1)
    mv_ref[...] = z[:, BRANCH_WIDTH:]


def _memkv(mem, nmem, w_kv, gk, tm):
    t, d = mem.shape
    w = BRANCH_WIDTH
    return pl.pallas_call(
        _memkv_kernel,
        grid=(t // tm,),
        in_specs=[pl.BlockSpec((tm, d), lambda i: (i, 0)), _const_spec(nmem.shape), _const_spec(w_kv.shape),
                  _const_spec(gk.shape)],
        out_specs=[pl.BlockSpec((tm, w), lambda i: (i, 0))] * 2,
        out_shape=[jax.ShapeDtypeStruct((t, w), F32)] * 2,
        compiler_params=_cparams("arbitrary"),
        name="memkv",
    )(mem, nmem, w_kv, gk)


def _memattn_kernel(q_ref, k_ref, v_ref, o_ref):
    for b in range(q_ref.shape[0]):
        outs = []
        for h in range(MEM_HEADS):
            sl = slice(MEM_HEAD_DIM * h, MEM_HEAD_DIM * (h + 1))
            q = q_ref[b][:, sl].astype(BF16)
            if len(k_ref.shape) == 4:
                k = k_ref[b, :, h, :].astype(BF16)
                v = v_ref[b, :, h, :].astype(BF16)
            else:
                k = k_ref[b][:, sl].astype(BF16)
                v = v_ref[b][:, sl].astype(BF16)
            s = _dot_nt(q, k) * MEM_SCALE
            e = jnp.exp(s - jnp.max(s, axis=1, keepdims=True))
            outs.append(_dot(e.astype(BF16), v) / jnp.sum(e, axis=1, keepdims=True))
        o_ref[b] = jnp.concatenate(outs, axis=1)


def _memattn(q, k, v, tq, nb=1):
    bm, tm, w = q.shape
    kv_spec = pl.BlockSpec((nb,) + k.shape[1:], lambda b, i: (b,) + (0,) * (k.ndim - 1))
    return pl.pallas_call(
        _memattn_kernel,
        grid=(bm // nb, tm // tq),
        in_specs=[pl.BlockSpec((nb, tq, w), lambda b, i: (b, i, 0)), kv_spec, kv_spec],
        out_specs=pl.BlockSpec((nb, tq, w), lambda b, i: (b, i, 0)),
        out_shape=jax.ShapeDtypeStruct((bm, tm, w), F32),
        compiler_params=_cparams("arbitrary", "arbitrary"),
        name="memattn",
    )(q, k, v)


def _merge_kernel(x_ref, ofox_ref, od_ref, omem_ref, gates_ref, dlam_ref, gdo_ref,
                  wpf_ref, wpd_ref, wpm_ref, wout_ref, nffn_ref, wq_ref, subk_ref,
                  h_ref, xnt_ref, st_ref, *, d_model, lam_init, attn_t):
    w = BRANCH_WIDTH
    dl = dlam_ref[...]
    lam = (jnp.exp(jnp.sum(dl[0:1] * dl[1:2], axis=1, keepdims=True))
           - jnp.exp(jnp.sum(dl[2:3] * dl[3:4], axis=1, keepdims=True)) + lam_init)
    if attn_t:
        od = od_ref[0].T
        ofox = ofox_ref[0].T
    else:
        od = od_ref[...]
        ofox = ofox_ref[...]
    o = od[:, :w] - lam * od[:, w:]
    hw = 2 * HEAD_DIM
    odn = jnp.concatenate(
        [_rms_scale(o[:, hw * h:hw * (h + 1)]) * gdo_ref[...] * (1.0 - lam_init) for h in range(DIFF_HEADS)], axis=1)
    b_fox = _dot(ofox.astype(BF16), wpf_ref[...])
    b_diff = _dot(odn.astype(BF16), wpd_ref[...])
    b_mem = _dot(omem_ref[...].astype(BF16), wpm_ref[...])
    gates = gates_ref[...]
    m = (gates[:, :d_model] * b_fox + gates[:, d_model:2 * d_model] * b_diff + gates[:, 2 * d_model:] * b_mem)
    h = x_ref[...] + _dot(m.astype(BF16), wout_ref[...])
    h_ref[...] = h
    xn = _rms_scale(h) * nffn_ref[...]
    xnb = xn.astype(BF16)
    xnt_ref[...] = xn.T.astype(BF16)
    q = _dot(xnb, wq_ref[...])
    for hc in range(2 * PEER_HEADS):
        qhc = q[:, N_KEYS * hc:N_KEYS * (hc + 1)].astype(BF16)
        st_ref[hc] = _dot_nt(subk_ref[hc % 2], qhc)


def _merge(x, ofox, od, omem, gates, weights, tt):
    t, d = x.shape
    w = BRANCH_WIDTH
    consts = [weights[k] for k in ("dlam", "gdo", "wpf", "wpd", "wpm", "wout", "nffn", "wq", "subk")]
    rows = lambda width: pl.BlockSpec((tt, width), lambda i: (i, 0))
    attn_t = ofox.ndim == 3
    if attn_t:
        nts = ofox.shape[2] // tt
        attn = lambda width: pl.BlockSpec((1, width, tt), lambda i: (i // nts, 0, i % nts))
    else:
        attn = rows
    return pl.pallas_call(
        functools.partial(_merge_kernel, d_model=d, lam_init=weights["lam_init"], attn_t=attn_t),
        grid=(t // tt,),
        in_specs=[rows(d), attn(w), attn(2 * w), rows(w), rows(N_BRANCH * d)] + [_const_spec(c.shape) for c in consts],
        out_specs=[rows(d), pl.BlockSpec((d, tt), lambda i: (0, i)),
                   pl.BlockSpec((2 * PEER_HEADS, N_KEYS, tt), lambda i: (0, 0, i))],
        out_shape=[jax.ShapeDtypeStruct((t, d), F32), jax.ShapeDtypeStruct((d, t), BF16),
                   jax.ShapeDtypeStruct((2 * PEER_HEADS, N_KEYS, t), F32)],
        compiler_params=_cparams("arbitrary"),
        name="merge",
    )(x, ofox, od, omem, gates, *consts)


def _oddeven_merge(lo, hi, r):
    step = r * 2
    if step < hi - lo:
        yield from _oddeven_merge(lo, hi, step)
        yield from _oddeven_merge(lo + r, hi, step)
        yield from [(i, i + r) for i in range(lo + r, hi - r, step)]
    else:
        yield (lo, lo + r)


def _oddeven_merge_sort(lo, hi):
    if hi - lo >= 1:
        mid = lo + (hi - lo) // 2
        yield from _oddeven_merge_sort(lo, mid)
        yield from _oddeven_merge_sort(mid + 1, hi)
        yield from _oddeven_merge(lo, hi, 1)


_SORT16 = tuple(_oddeven_merge_sort(0, PEER_TOPK - 1))


def _cmpx(v, i, j):
    a, b = v[i], v[j]
    if b is None:
        return
    if a is None:
        v[i], v[j] = b, None
        return
    v[i], v[j] = jnp.maximum(a, b), jnp.minimum(a, b)


def _top16_replicated(v):
    v = list(v)
    for i, j in _SORT16:
        _cmpx(v, i, j)
    for shift in (4, 2, 1):
        part = [None if a is None else pltpu.roll(a, shift, 0) for a in v]
        merged = []
        for i in range(PEER_TOPK):
            a, b = v[i], part[PEER_TOPK - 1 - i]
            merged.append(b if a is None else a if b is None else jnp.maximum(a, b))
        for stride in (8, 4, 2, 1):
            for i in range(PEER_TOPK):
                if not i & stride:
                    _cmpx(merged, i, i + stride)
        v = merged
    return v


def _topk_kernel(st_ref, e1_ref, e2_ref, thr_ref, *, tl):
    sub = lax.broadcasted_iota(jnp.int32, (SUBLANES, tl), 0)

    def spread(vals):
        out = vals[SUBLANES - 1]
        for s in range(SUBLANES - 2, -1, -1):
            out = jnp.where(sub == s, vals[s], out)
        return out

    def body(h, carry):
        s1 = st_ref[2 * h]
        s2 = st_ref[2 * h + 1]
        a = _top16_replicated([s1[SUBLANES * i:SUBLANES * (i + 1)] for i in range(N_KEYS // SUBLANES)])
        b = _top16_replicated([s2[SUBLANES * i:SUBLANES * (i + 1)] for i in range(N_KEYS // SUBLANES)])
        b_lo, b_hi = spread(b[:SUBLANES]), spread(b[SUBLANES:])
        cand = [a[0] + b_lo, a[0] + b_hi] + [a[i] + b_lo for i in range(1, SUBLANES)] + [spread(a[SUBLANES:]) + b[0]]
        best = _top16_replicated(cand + [None] * (PEER_TOPK - len(cand)))
        z = jnp.exp(best[0] - best[0])
        for kk in range(1, PEER_TOPK):
            z = z + jnp.exp(best[kk] - best[0])
        reps = N_KEYS // SUBLANES
        tall = lambda v: jnp.concatenate([v] * reps, axis=0)
        e1_ref[h] = jnp.exp(s1 - tall(a[0]))
        e2_ref[h] = jnp.exp(s2 - tall(b[0])) / tall(z)
        tau = tall(best[PEER_TOPK - 1])
        n = jnp.zeros((N_KEYS, tl), F32)
        for jj in range(PEER_TOPK):
            n = n + jnp.where(s1 + tall(b[jj]) >= tau, 1.0, 0.0)
        thr = jnp.full((N_KEYS, tl), jnp.inf, F32)
        for jj in range(PEER_TOPK):
            thr = jnp.where(n > jj, tall(b[jj]), thr)
        thr_ref[h] = thr
        return carry

    lax.fori_loop(0, PEER_HEADS, body, 0)


def _topk(st, tl):
    nhc, nk, t = st.shape
    blk = lambda n: pl.BlockSpec((n, nk, tl), lambda i: (0, 0, i))
    return pl.pallas_call(
        functools.partial(_topk_kernel, tl=tl),
        grid=(t // tl,),
        in_specs=[blk(nhc)],
        out_specs=[blk(PEER_HEADS)] * 3,
        out_shape=[jax.ShapeDtypeStruct((PEER_HEADS, nk, t), F32)] * 3,
        compiler_params=_cparams("arbitrary"),
        name="topk",
    )(st)


def _peer_kernel(xt_ref, u_ref, vt_ref, st_ref, e1_ref, e2_ref, thr_ref, h_ref, y_ref, acc_ref, *, eb, tt):
    ej = pl.program_id(1)

    @pl.when(ej == 0)
    def _():
        acc_ref[...] = jnp.zeros_like(acc_ref)

    ht = _dot(u_ref[...], xt_ref[...])
    acts = []
    for j in range(eb // N_KEYS):
        a = ej * (eb // N_KEYS) + j
        wt = jnp.zeros((N_KEYS, tt), F32)
        for h in range(PEER_HEADS):
            val = e1_ref[h, pl.ds(a, 1), :] * e2_ref[h]
            wt = wt + jnp.where(st_ref[2 * h + 1] >= thr_ref[h, pl.ds(a, 1), :], val, 0.0)
        hj = ht[N_KEYS * j:N_KEYS * (j + 1)]
        gelu = 0.5 * hj * (1.0 + lax.erf(hj * (2.0 ** -0.5)))
        acts.append((wt * gelu).astype(BF16))
    acc_ref[...] += _dot(vt_ref[...], jnp.concatenate(acts, axis=0))

    @pl.when(ej == pl.num_programs(1) - 1)
    def _():
        y_ref[...] = h_ref[...] + acc_ref[...].T


def _peer(xt, u, vt, st, e1, e2, tau, h, tt, eb):
    d, t = xt.shape
    ne = u.shape[0]
    tok3 = lambda n: pl.BlockSpec((n, N_KEYS, tt), lambda i, j: (0, 0, i))
    return pl.pallas_call(
        functools.partial(_peer_kernel, eb=eb, tt=tt),
        grid=(t // tt, ne // eb),
        in_specs=[pl.BlockSpec((d, tt), lambda i, j: (0, i)),
                  pl.BlockSpec((eb, d), lambda i, j: (j, 0)),
                  pl.BlockSpec((d, eb), lambda i, j: (0, j)),
                  tok3(2 * PEER_HEADS), tok3(PEER_HEADS), tok3(PEER_HEADS), tok3(PEER_HEADS),
                  pl.BlockSpec((tt, d), lambda i, j: (i, 0))],
        out_specs=pl.BlockSpec((tt, d), lambda i, j: (i, 0)),
        out_shape=jax.ShapeDtypeStruct((t, d), F32),
        scratch_shapes=[pltpu.VMEM((d, tt), F32)],
        compiler_params=_cparams("arbitrary", "arbitrary"),
        name="peer",
    )(xt, u, vt, st, e1, e2, tau, h)


def _paged_kernel(pt_ref, q_ref, knew_ref, vnew_ref, lg_ref, *rest, fox, n_pages, pages_per_step, dec_seq, past_len):
    npp = pages_per_step
    if not fox:
        spread_ref, rest = rest[0], rest[1:]
    k_refs = rest[:npp]
    v_refs = rest[npp:2 * npp]
    rest = rest[2 * npp:]
    lf_refs = rest[:npp] if fox else ()
    o_ref, m_ref, l_ref, acc_ref, carry_ref = rest[-5:]
    j = pl.program_id(1)
    rows = dec_seq * SCORE_HEADS
    q = q_ref[0]
    r_iota = lax.broadcasted_iota(jnp.int32, (rows, 1), 0)
    if fox:
        r_t = r_iota // SCORE_HEADS
        lg = jnp.concatenate([lg_ref[0]] * dec_seq, axis=0)
        n_row = jnp.zeros((rows, 1), F32)
        for u in range(dec_seq):
            n_row = n_row + jnp.where(r_t >= u, lg[:, u:u + 1], 0.0)
    else:
        r_t = r_iota % dec_seq
        r_h = r_iota // (2 * dec_seq)
        slope = jnp.where(r_h == 0, 2.0 ** -2, jnp.where(r_h == 1, 2.0 ** -4, jnp.where(r_h == 2, 2.0 ** -6, 2.0 ** -8)))

    def head_rows(p, h):
        return p[SUBLANES * h:SUBLANES * (h + 1)]

    def update(scores, weigh):
        m_prev = m_ref[...]
        m_new = m_prev
        for s in scores:
            m_new = jnp.maximum(m_new, jnp.max(s, axis=1, keepdims=True))
        alpha = jnp.exp(m_prev - m_new)
        l_new = alpha * l_ref[...]
        acc = alpha * acc_ref[...]
        for s, pv in zip(scores, weigh):
            pexp = jnp.exp(s - m_new)
            l_new = l_new + jnp.sum(pexp, axis=1, keepdims=True)
            acc = acc + pv(pexp.astype(BF16))
        m_ref[...] = m_new
        l_ref[...] = l_new
        acc_ref[...] = acc

    @pl.when(j == 0)
    def _():
        m_ref[...] = jnp.full_like(m_ref, NEG_INF)
        l_ref[...] = jnp.zeros_like(l_ref)
        acc_ref[...] = jnp.zeros_like(acc_ref)
        carry_ref[...] = jnp.zeros_like(carry_ref)
        s = _dot_nt(q, knew_ref[0])
        col = lax.broadcasted_iota(jnp.int32, (1, s.shape[1]), 1)
        if fox:
            n_col = jnp.zeros(s.shape, F32)
            for u in range(dec_seq):
                n_col = n_col + jnp.where(col >= u, lg[:, u:u + 1], 0.0)
            s = s + (n_row - n_col)
            pv = lambda p: _dot(p, vnew_ref[0])
        else:
            s = s - slope * (r_t - col).astype(F32)
            pv = lambda p: jnp.concatenate([_dot(head_rows(p, h), vnew_ref[0, h]) for h in range(DIFF_HEADS)], axis=0)
        update([jnp.where(col <= r_t, s, NEG_INF)], [pv])

    @pl.when(j > 0)
    def _():
        tok = lax.broadcasted_iota(jnp.int32, (1, PAGE_SIZE), 1)
        if fox:
            ri = lax.broadcasted_iota(jnp.int32, (PAGE_SIZE, PAGE_SIZE), 0)
            ci = lax.broadcasted_iota(jnp.int32, (PAGE_SIZE, PAGE_SIZE), 1)
            later = jnp.where(ri > ci, 1.0, 0.0).astype(BF16)
        else:
            vcol = lax.broadcasted_iota(jnp.int32, (1, PAGE_SIZE * DIFF_HEADS), 1)
            own_head = (vcol % DIFF_HEADS) == r_h
        scores, weigh = [], []
        for i in range(npp):
            slot = n_pages - 1 - ((j - 1) * npp + i)
            s = _dot(q, k_refs[i][0].astype(BF16))
            if fox:
                lf = lf_refs[i][0]
                c = _dot(jnp.concatenate(_split3(lf), axis=0), later)
                after = c[:FOX_HEADS] + c[FOX_HEADS:2 * FOX_HEADS] + c[2 * FOX_HEADS:] + carry_ref[...]
                carry_ref[...] = after[:, 0:1] + lf[:, 0:1]
                s = s + (jnp.concatenate([after] * dec_seq, axis=0) + n_row)
                weigh.append(lambda p, v_ref=v_refs[i]: _dot_nt(p, v_ref[0].astype(BF16)))
            else:
                s = s - slope * (past_len + r_t - slot * PAGE_SIZE - tok).astype(F32)
                def pv(p, v_ref=v_refs[i]):
                    v = v_ref[0].reshape(PAGE_SIZE * DIFF_HEADS, 2 * HEAD_DIM).astype(BF16)
                    return _dot(jnp.where(own_head, _dot(p, spread_ref[...]), 0.0).astype(BF16), v)
                weigh.append(pv)
            scores.append(s)
        update(scores, weigh)

    @pl.when(j == pl.num_programs(1) - 1)
    def _():
        o_ref[0] = acc_ref[...] / l_ref[...]


def _paged(page_table, q, knew, vnew, lg, kt_pool, v_pool, lf_pool, *, fox, pages_per_step, dec_seq):
    b, n_pages = page_table.shape
    npp = pages_per_step
    rows = q.shape[1]
    dv = BRANCH_WIDTH if fox else 2 * HEAD_DIM
    pt = page_table.reshape(-1)

    def page_map(i, nd):
        def index(bi, j, pt_ref):
            slot = n_pages - 1 - (jnp.maximum(j, 1) - 1) * npp - i
            return (pt_ref[bi * n_pages + slot],) + (0,) * nd
        return index

    def per_seq(a):
        nd = a.ndim - 1
        return pl.BlockSpec((1,) + a.shape[1:], lambda bi, j, pt_ref: (bi,) + (0,) * nd)

    def paged(a):
        return [pl.BlockSpec((1,) + a.shape[1:], page_map(i, a.ndim - 1)) for i in range(npp)]

    in_specs = [per_seq(a) for a in (q, knew, vnew, lg)]
    args = [q, knew, vnew, lg]
    if not fox:
        tok = np.arange(PAGE_SIZE)[:, None]
        col = np.arange(PAGE_SIZE * DIFF_HEADS)[None, :]
        spread = jnp.asarray(col // DIFF_HEADS == tok, BF16)
        in_specs.append(pl.BlockSpec(spread.shape, lambda bi, j, pt_ref: (0, 0), pipeline_mode=pl.Buffered(1)))
        args.append(spread)
    in_specs += paged(kt_pool) + paged(v_pool)
    args += [kt_pool] * npp + [v_pool] * npp
    if fox:
        in_specs += paged(lf_pool)
        args += [lf_pool] * npp
    grid_spec = pltpu.PrefetchScalarGridSpec(
        num_scalar_prefetch=1,
        grid=(b, n_pages // npp + 1),
        in_specs=in_specs,
        out_specs=pl.BlockSpec((1, rows, dv), lambda bi, j, pt_ref: (bi, 0, 0)),
        scratch_shapes=[pltpu.VMEM((rows, 1), F32), pltpu.VMEM((rows, 1), F32), pltpu.VMEM((rows, dv), F32),
                        pltpu.VMEM((SCORE_HEADS, 1), F32)],
    )
    return pl.pallas_call(
        functools.partial(_paged_kernel, fox=fox, n_pages=n_pages, pages_per_step=npp, dec_seq=dec_seq,
                          past_len=n_pages * PAGE_SIZE),
        grid_spec=grid_spec,
        out_shape=jax.ShapeDtypeStruct((b, rows, dv), F32),
        compiler_params=_cparams("arbitrary", "arbitrary"),
        name="paged_fox" if fox else "paged_diff",
    )(pt, *args)


def _tile(n, target):
    t = min(n, target)
    while n % t:
        t -= 1
    return t


def _layer_weights(layer, d, norm_mix, norm_mem, w_in, b_f, fox_q_norm, fox_k_norm, diff_q_norm, diff_k_norm,
                   diff_lambda, diff_out_norm, w_mem_kv, mem_q_norm, mem_k_norm, w_proj_fox, w_proj_diff,
                   w_proj_mem, w_out, norm_ffn, w_peer_q, peer_subkeys, peer_u, peer_v):
    w = BRANCH_WIDTH
    l = layer
    win = w_in[l]
    ff0 = 3 * w
    pad_lanes = lambda a: jnp.pad(a, ((0, 0), (0, LANES - a.shape[1])))
    pqf, pkf, pqd, pkd = _placement_matrices()
    bd64 = jnp.asarray(np.kron(np.eye(SCORE_HEADS), np.full((HEAD_DIM, HEAD_DIM), 1.0 / HEAD_DIM)), BF16)
    tile8 = lambda g: jnp.tile(g[l], SCORE_HEADS)[None, :]
    return dict(
        nmix=norm_mix[l][None, :],
        wmain=jnp.concatenate([win[:, :ff0], win[:, ff0 + FOX_HEADS:]], axis=1).astype(BF16),
        wff=pad_lanes(win[:, ff0:ff0 + FOX_HEADS]).astype(BF16),
        bf=pad_lanes(b_f[l][None, :]),
        bd64=bd64,
        gfq=tile8(fox_q_norm), gfk=tile8(fox_k_norm), gdq=tile8(diff_q_norm), gdk=tile8(diff_k_norm),
        gmq=mem_q_norm[l][None, :],
        pqf=pqf, pkf=pkf, pqd=pqd, pkd=pkd,
        nmem=norm_mem[l][None, :], wmemkv=w_mem_kv[l].astype(BF16), gmk=mem_k_norm[l][None, :],
        dlam=diff_lambda[l], gdo=diff_out_norm[l][None, :],
        wpf=w_proj_fox[l].astype(BF16), wpd=w_proj_diff[l].astype(BF16), wpm=w_proj_mem[l].astype(BF16),
        wout=w_out[l].astype(BF16), nffn=norm_ffn[l][None, :], wq=w_peer_q[l].astype(BF16),
        subk=peer_subkeys[l].astype(BF16),
        u=peer_u[l].astype(BF16), vt=peer_v[l].T.astype(BF16),
        lam_init=0.8 - 0.6 * float(np.exp(-0.3 * l)),
    )


def _ffn_tail(x, ofox, od, omem, gates, wts):
    t, d = x.shape
    h, xnt, st = _merge(x, ofox, od, omem, gates, wts, _tile(t, 256))
    e1, e2, tau = _topk(st, _tile(t, 256))
    return _peer(xnt, wts["u"], wts["vt"], st, e1, e2, tau, h, _tile(t, 512), _tile(wts["u"].shape[0], 1024))


def kernel(x_prompt, x_sample, mem_prompt, cache_fox_k, cache_fox_v, cache_fox_logf, cache_diff_k, cache_diff_v, cache_mem_k, cache_mem_v, page_table, norm_mix, norm_mem, w_in, b_f, fox_q_norm, fox_k_norm, diff_q_norm, diff_k_norm, diff_lambda, diff_out_norm, w_mem_kv, mem_q_norm, mem_k_norm, w_proj_fox, w_proj_diff, w_proj_mem, w_out, norm_ffn, w_peer_q, peer_subkeys, peer_u, peer_v):
    depth = w_in.shape[0]
    b, s, d = x_prompt.shape
    db, ds, _ = x_sample.shape
    n_mem = mem_prompt.shape[1]
    n_pool = cache_fox_k.shape[1]
    w = BRANCH_WIDTH
    xp, xs = x_prompt, x_sample
    outs = [[] for _ in range(12)]
    for l in range(depth):
        wts = _layer_weights(l, d, norm_mix, norm_mem, w_in, b_f, fox_q_norm, fox_k_norm, diff_q_norm, diff_k_norm,
                             diff_lambda, diff_out_norm, w_mem_kv, mem_q_norm, mem_k_norm, w_proj_fox, w_proj_diff,
                             w_proj_mem, w_out, norm_ffn, w_peer_q, peer_subkeys, peer_u, peer_v)

        pr = _proj(xp, wts, _tile(s, 256), LOG2E)
        mk, mv = _memkv(mem_prompt.reshape(b * n_mem, d), wts["nmem"], wts["wmemkv"], wts["gmk"], _tile(b * n_mem, 256))
        tq, tk = _tile(s, FLASH_TQ), _tile(s, FLASH_TK)
        ofox = _flash(pr["qf"], pr["kf"], pr["vfb"], group=2, dv=HEAD_DIM, tq=tq, tk=tk, name="flash_fox")
        od = _flash(pr["qd"], pr["kd"], pr["vdb"], group=2, dv=2 * HEAD_DIM, tq=tq, tk=tk, name="flash_diff")
        omem = _memattn(pr["mq"], mk.reshape(b, n_mem, w), mv.reshape(b, n_mem, w), _tile(s, 512))
        xp = _ffn_tail(xp.reshape(b * s, d), ofox, od, omem.reshape(b * s, w),
                       pr["gates"].reshape(b * s, N_BRANCH * d), wts).reshape(b, s, d)
        outs[0].append(pr["fk"].reshape(b, s, FOX_HEADS, HEAD_DIM))
        outs[1].append(pr["fv"].reshape(b, s, FOX_HEADS, HEAD_DIM))
        outs[2].append(pr["logf"])
        outs[3].append(pr["dk"].reshape(b, s, 2 * DIFF_HEADS, HEAD_DIM))
        outs[4].append(pr["dv"].reshape(b, s, DIFF_HEADS, 2 * HEAD_DIM))
        outs[5].append(mk.reshape(b, n_mem, MEM_HEADS, MEM_HEAD_DIM))
        outs[6].append(mv.reshape(b, n_mem, MEM_HEADS, MEM_HEAD_DIM))

        ts = db * ds
        sr = _proj(xs.reshape(1, ts, d), wts, _tile(ts, 256), 1.0)
        pad_new = SUBLANES - ds
        pad_new_bf16 = 2 * SUBLANES - ds
        assert 2 * ds == SUBLANES, "the differential decode rows put one value head's (map, token) rows in one tile"
        nr = ds * SCORE_HEADS
        eye = jnp.eye(SCORE_HEADS, dtype=BF16)

        def block_diag_q(qp, order):
            qh = qp[0, :, :, :HEAD_DIM].reshape(SCORE_HEADS, db, ds, HEAD_DIM).transpose(1, 2, 0, 3)
            qbd = qh[:, :, :, None, :] * eye[None, None, :, :, None]
            return order(qbd).reshape(db, nr, w)

        def new_rows(a):
            return jnp.pad(a.reshape(db, ds, w), ((0, 0), (0, pad_new_bf16), (0, 0))).astype(BF16)

        def key_major(c):
            return c.transpose(0, 2, 3, 1).reshape(n_pool, w, PAGE_SIZE)

        lg = sr["logf"].reshape(db, ds, FOX_HEADS).transpose(0, 2, 1)
        npp = _tile(page_table.shape[1], PAGES_PER_STEP)
        ofs = _paged(page_table, block_diag_q(sr["qf"], lambda x: x), new_rows(sr["fk"]), new_rows(sr["fv"]), lg,
                     key_major(cache_fox_k[l]), key_major(cache_fox_v[l]), cache_fox_logf[l].transpose(0, 2, 1),
                     fox=True, pages_per_step=npp, dec_seq=ds)
        hidx = jnp.arange(FOX_HEADS)
        ofs = ofs.reshape(db, ds, FOX_HEADS, FOX_HEADS, HEAD_DIM)[:, :, hidx, hidx, :].reshape(ts, w)
        by_head = lambda x: x.reshape(db, ds, 2, DIFF_HEADS, SCORE_HEADS, HEAD_DIM).transpose(0, 3, 2, 1, 4, 5)
        dv_new = jnp.pad(sr["dv"].reshape(db, ds, DIFF_HEADS, 2 * HEAD_DIM).transpose(0, 2, 1, 3),
                         ((0, 0), (0, 0), (0, pad_new_bf16), (0, 0))).astype(BF16)
        ods = _paged(page_table, block_diag_q(sr["qd"], by_head), new_rows(sr["dk"]), dv_new, lg,
                     key_major(cache_diff_k[l]), cache_diff_v[l], None, fox=False, pages_per_step=npp, dec_seq=ds)
        ods = ods.reshape(db, DIFF_HEADS, 2, ds, 2 * HEAD_DIM).transpose(0, 3, 2, 1, 4).reshape(ts, 2 * w)
        mqs = jnp.pad(sr["mq"].reshape(db, ds, w), ((0, 0), (0, pad_new), (0, 0)))
        oms = _memattn(mqs, cache_mem_k[l], cache_mem_v[l], SUBLANES, _tile(db, 8))
        oms = oms[:, :ds].reshape(ts, w)
        xs = _ffn_tail(xs.reshape(ts, d), ofs, ods, oms, sr["gates"].reshape(ts, N_BRANCH * d), wts).reshape(db, ds, d)
        outs[7].append(sr["fk"].reshape(db, ds, FOX_HEADS, HEAD_DIM))
        outs[8].append(sr["fv"].reshape(db, ds, FOX_HEADS, HEAD_DIM))
        outs[9].append(sr["logf"].reshape(db, ds, FOX_HEADS))
        outs[10].append(sr["dk"].reshape(db, ds, 2 * DIFF_HEADS, HEAD_DIM))
        outs[11].append(sr["dv"].reshape(db, ds, DIFF_HEADS, 2 * HEAD_DIM))

    return (xp, xs) + tuple(jnp.stack(o) for o in outs)
```
